```python
import math
import jax, jax.numpy as jnp
from jax import lax
import numpy as np

D_MODEL = 2048
BATCH = 4
SEQ = 2048
DEPTH = 2
DEC_BATCH = 8
DEC_SEQ = 2048
PAST_LEN = 128

GRID_W = 64
N_EVEN = (DEPTH + 1) // 2
N_ODD = DEPTH // 2
NORM_EPS = 1e-6
D_FF = 5504
HY_D = D_MODEL // 2
HY_ORDER = 2
HY_SHORT_W = 3
HY_EMB_DIM = 33
HY_BANDS = (HY_EMB_DIM - 1) // 2
HY_FILT_HID = 64
HY_N_INNER = 2
HY_FAST_DECAY = 0.3
HY_SLOW_DECAY = 1.5
HY_DECAY_TARGET = 1e-2
HEAD_DIM = 128
N_Q_HEADS = (D_MODEL // 2) // HEAD_DIM
N_KV_HEADS = 2
GQA_GROUP = N_Q_HEADS // N_KV_HEADS
Q_BLOCK = 128
ROPE_THETA = 10000.0
ROPE_AXIS_DIM = HEAD_DIM // 2
D_RNN = D_MODEL
RNN_HEADS = 8
RNN_BW = D_RNN // RNN_HEADS
RNN_CONV_W = 4
RNN_C = 8.0

IN_EVEN = (HY_ORDER + 1) * HY_D + (N_Q_HEADS + 2 * N_KV_HEADS) * HEAD_DIM
MIX_EVEN = HY_D + N_Q_HEADS * HEAD_DIM

kernel_name = 'hyena_gqa_rglru_macaron_encoder'

F32 = jnp.float32


def _rmsnorm(x, g):
    xf = x.astype(F32)
    xf = xf * lax.rsqrt(jnp.mean(xf * xf, axis=-1, keepdims=True) + NORM_EPS)
    return (xf * g.astype(F32)).astype(x.dtype)


def _swiglu(x, wg, wu, wd):
    return (jax.nn.silu(x @ wg) * (x @ wu)) @ wd


def _hyena_filters(L, w1, b1, w2, b2, w3, freq):
    pos = jnp.arange(L, dtype=F32)
    t = pos / (L - 1)
    w = 2.0 * math.pi * pos / L
    f = jnp.linspace(1e-4, HY_BANDS - 1, HY_BANDS, dtype=F32)
    fw = w[:, None] * f[None, :]
    z = jnp.concatenate([t[:, None], jnp.cos(fw), -jnp.sin(fw)], axis=-1)
    h = jnp.sin(freq * (z @ w1 + b1))
    for i in range(HY_N_INNER):
        h = jnp.sin(freq * (h @ w2[i] + b2[i]))
    k = (h @ w3).astype(F32).reshape(L, 2, HY_ORDER, HY_D)
    max_decay = math.log(HY_DECAY_TARGET) / HY_FAST_DECAY
    min_decay = math.log(HY_DECAY_TARGET) / HY_SLOW_DECAY
    deltas = jnp.linspace(min_decay, max_decay, HY_D, dtype=F32)
    decay = jnp.exp(-t[:, None] * jnp.abs(deltas)[None, :])
    k = k * decay[:, None, None, :]
    kc = jnp.concatenate([k[:, 0], jnp.zeros((1, HY_ORDER, HY_D), F32), k[:0:-1, 1]], axis=0)
    return jnp.fft.rfft(kc, axis=0)


def _short_conv_centred(u, w, b):
    L = u.shape[1]
    half = HY_SHORT_W // 2
    up = jnp.pad(u, ((0, 0), (half, half), (0, 0)))
    y = b
    for j in range(HY_SHORT_W):
        y = y + up[:, j:j + L] * w[j]
    return y


def _fftconv(z, kf, bias):
    L = z.shape[1]
    zf = jnp.fft.rfft(z, n=2 * L, axis=1)
    y = jnp.fft.irfft(zf * kf[None], n=2 * L, axis=1)[:, :L]
    return y + z * bias


def _hyena(u, short_w, short_b, f_w1, f_b1, f_w2, f_b2, f_w3, f_freq, hy_bias):
    L = u.shape[1]
    u = _short_conv_centred(u, short_w, short_b).astype(F32)
    v, x1, x2 = jnp.split(u, 3, axis=-1)
    kf = _hyena_filters(L, f_w1, f_b1, f_w2, f_b2, f_w3, f_freq)
    z = v
    for o, gate in enumerate((x1, x2)):
        z = gate * _fftconv(z, kf[:, o], hy_bias[o].astype(F32))
    return z


def _axial_rope_angles(L):
    rows = L // GRID_W
    row = jnp.repeat(jnp.arange(rows, dtype=F32), GRID_W)
    col = jnp.tile(jnp.arange(GRID_W, dtype=F32), rows)
    omega = ROPE_THETA ** (-jnp.arange(0, ROPE_AXIS_DIM, 2, dtype=F32) / ROPE_AXIS_DIM)
    ang = jnp.concatenate([row[:, None] * omega[None], col[:, None] * omega[None]], axis=-1)
    return jnp.cos(ang), jnp.sin(ang)


def _rope(x, cos, sin):
    xp = x.reshape(x.shape[:-1] + (HEAD_DIM // 2, 2))
    x0, x1 = xp[..., 0], xp[..., 1]
    c = cos[None, :, None, :]
    s = sin[None, :, None, :]
    return jnp.stack([x0 * c - x1 * s, x0 * s + x1 * c], axis=-1).reshape(x.shape)


def _attention(q, k, v, q_g, k_g):
    B, L = q.shape[:2]
    cos, sin = _axial_rope_angles(L)
    q = _rope(_rmsnorm(q, q_g).astype(F32), cos, sin) * (HEAD_DIM ** -0.5)
    k = _rope(_rmsnorm(k, k_g).astype(F32), cos, sin)
    nb = L // Q_BLOCK
    qb = q.reshape(B, nb, Q_BLOCK, N_KV_HEADS, GQA_GROUP, HEAD_DIM).transpose(1, 0, 2, 3, 4, 5)

    def block(qblk):
        s = jnp.einsum('bqhgd,bkhd->bhgqk', qblk, k)
        pr = jax.nn.softmax(s, axis=-1).astype(v.dtype)
        return jnp.einsum('bhgqk,bkhd->bqhgd', pr, v)

    o = lax.map(block, qb)
    return o.transpose(1, 0, 2, 3, 4, 5).reshape(B, L, N_Q_HEADS * HEAD_DIM)


def _even_mixer(h, w_in, short_w, short_b, f_w1, f_b1, f_w2, f_b2, f_w3, f_freq, hy_bias, q_g, k_g, w_out):
    B, L, _ = h.shape
    u = h @ w_in
    s0 = (HY_ORDER + 1) * HY_D
    s1 = s0 + N_Q_HEADS * HEAD_DIM
    s2 = s1 + N_KV_HEADS * HEAD_DIM
    u_hy, q, k, v = jnp.split(u, [s0, s1, s2], axis=-1)
    y_hy = _hyena(u_hy, short_w, short_b, f_w1, f_b1, f_w2, f_b2, f_w3, f_freq, hy_bias).astype(h.dtype)
    y_at = _attention(q.reshape(B, L, N_Q_HEADS, HEAD_DIM), k.reshape(B, L, N_KV_HEADS, HEAD_DIM),
                      v.reshape(B, L, N_KV_HEADS, HEAD_DIM), q_g, k_g).astype(h.dtype)
    return jnp.concatenate([y_hy, y_at], axis=-1) @ w_out


def _lin_combine(e1, e2):
    a1, b1 = e1
    a2, b2 = e2
    return a1 * a2, a2 * b1 + b2


def _rglru_forward(x, conv_w, conv_b, wa, ba, wx, bx, lam):
    B, L, _ = x.shape
    xp = jnp.pad(x, ((0, 0), (RNN_CONV_W - 1, 0), (0, 0)))
    xc = conv_b
    for j in range(RNN_CONV_W):
        xc = xc + xp[:, j:j + L] * conv_w[j]
    xr = xc.reshape(B, L, RNN_HEADS, RNN_BW)
    r = jax.nn.sigmoid(jnp.einsum('blhi,hij->blhj', xr, wa) + ba).reshape(B, L, D_RNN)
    i = jax.nn.sigmoid(jnp.einsum('blhi,hij->blhj', xr, wx) + bx).reshape(B, L, D_RNN)
    log_a = -RNN_C * r * jax.nn.softplus(-lam)
    a = jnp.exp(log_a)
    mult = jnp.sqrt(-jnp.expm1(2.0 * log_a))
    mult = mult.at[:, 0].set(1.0)
    b = mult * (i * xc)
    _, hs = lax.associative_scan(_lin_combine, (a, b), axis=1)
    return hs


def _odd_mixer(h, w_in, conv_w, conv_b, wa, ba, wx, bx, lam, w_out):
    u = h @ w_in
    g, xb = jnp.split(u, 2, axis=-1)
    xb = xb.astype(F32)
    h_f = _rglru_forward(xb, conv_w[0], conv_b[0], wa[0], ba[0], wx[0], bx[0], lam[0])
    h_b = _rglru_forward(xb[:, ::-1], conv_w[1], conv_b[1], wa[1], ba[1], wx[1], bx[1], lam[1])[:, ::-1]
    y = ((h_f + h_b) * jax.nn.gelu(g.astype(F32))).astype(h.dtype)
    return y @ w_out


def _trunk(x, p):
    for layer in range(DEPTH):
        x = x + 0.5 * _swiglu(_rmsnorm(x, p['ffn_norm'][layer, 0]), p['ffn_w_gate'][layer, 0],
                              p['ffn_w_up'][layer, 0], p['ffn_w_down'][layer, 0])
        j = layer // 2
        if layer % 2 == 0:
            x = x + _even_mixer(_rmsnorm(x, p['even_norm'][j]), p['even_w_in'][j], p['hy_short_w'][j],
                                p['hy_short_b'][j], p['hy_filt_w1'][j], p['hy_filt_b1'][j], p['hy_filt_w2'][j],
                                p['hy_filt_b2'][j], p['hy_filt_w3'][j], p['hy_filt_freq'][j], p['hy_bias'][j],
                                p['q_norm'][j], p['k_norm'][j], p['even_w_out'][j])
        else:
            x = x + _odd_mixer(_rmsnorm(x, p['odd_norm'][j]), p['odd_w_in'][j], p['rg_conv_w'][j],
                               p['rg_conv_b'][j], p['rg_wa'][j], p['rg_ba'][j], p['rg_wx'][j], p['rg_bx'][j],
                               p['rg_lambda'][j], p['odd_w_out'][j])
        x = x + 0.5 * _swiglu(_rmsnorm(x, p['ffn_norm'][layer, 1]), p['ffn_w_gate'][layer, 1],
                              p['ffn_w_up'][layer, 1], p['ffn_w_down'][layer, 1])
    return x


def setup_inputs(seed: int = 0) -> dict:
    key = jax.random.key(seed)
    ks = jax.random.split(key, 30)

    def nrm(k, shape, scale):
        return scale * jax.random.normal(k, shape, F32)

    def gain(k, shape):
        return 1.0 + 0.01 * jax.random.normal(k, shape, F32)

    a0 = jax.random.uniform(ks[28], (N_ODD, 2, D_RNN), F32, 0.9, 0.999)
    return {
        'x_prompt': nrm(ks[0], (BATCH, SEQ, D_MODEL), 1.0),
        'x_sample': nrm(ks[1], (DEC_BATCH, DEC_SEQ, D_MODEL), 1.0),
        'ffn_norm': gain(ks[2], (DEPTH, 2, D_MODEL)),
        'ffn_w_gate': nrm(ks[3], (DEPTH, 2, D_MODEL, D_FF), D_MODEL ** -0.5),
        'ffn_w_up': nrm(ks[4], (DEPTH, 2, D_MODEL, D_FF), D_MODEL ** -0.5),
        'ffn_w_down': nrm(ks[5], (DEPTH, 2, D_FF, D_MODEL), D_FF ** -0.5),
        'even_norm': gain(ks[6], (N_EVEN, D_MODEL)),
        'even_w_in': nrm(ks[7], (N_EVEN, D_MODEL, IN_EVEN), D_MODEL ** -0.5),
        'hy_short_w': nrm(ks[8], (N_EVEN, HY_SHORT_W, (HY_ORDER + 1) * HY_D), HY_SHORT_W ** -0.5),
        'hy_short_b': nrm(ks[9], (N_EVEN, (HY_ORDER + 1) * HY_D), 0.01),
        'hy_filt_w1': nrm(ks[10], (N_EVEN, HY_EMB_DIM, HY_FILT_HID), HY_EMB_DIM ** -0.5),
        'hy_filt_b1': nrm(ks[11], (N_EVEN, HY_FILT_HID), 0.1),
        'hy_filt_w2': nrm(ks[12], (N_EVEN, HY_N_INNER, HY_FILT_HID, HY_FILT_HID), HY_FILT_HID ** -0.5),
        'hy_filt_b2': nrm(ks[13], (N_EVEN, HY_N_INNER, HY_FILT_HID), 0.1),
        'hy_filt_w3': nrm(ks[14], (N_EVEN, HY_FILT_HID, 2 * HY_ORDER * HY_D), 0.05 * HY_FILT_HID ** -0.5),
        'hy_filt_freq': gain(ks[15], (N_EVEN, HY_FILT_HID)),
        'hy_bias': nrm(ks[16], (N_EVEN, HY_ORDER, HY_D), 1.0),
        'q_norm': gain(ks[17], (N_EVEN, HEAD_DIM)),
        'k_norm': gain(ks[18], (N_EVEN, HEAD_DIM)),
        'even_w_out': nrm(ks[19], (N_EVEN, MIX_EVEN, D_MODEL), MIX_EVEN ** -0.5),
        'odd_norm': gain(ks[20], (N_ODD, D_MODEL)),
        'odd_w_in': nrm(ks[21], (N_ODD, D_MODEL, 2 * D_RNN), D_MODEL ** -0.5),
        'rg_conv_w': nrm(ks[22], (N_ODD, 2, RNN_CONV_W, D_RNN), RNN_CONV_W ** -0.5),
        'rg_conv_b': nrm(ks[23], (N_ODD, 2, D_RNN), 0.01),
        'rg_wa': nrm(ks[24], (N_ODD, 2, RNN_HEADS, RNN_BW, RNN_BW), RNN_BW ** -0.5),
        'rg_ba': nrm(ks[25], (N_ODD, 2, RNN_HEADS, RNN_BW), 0.01),
        'rg_wx': nrm(ks[26], (N_ODD, 2, RNN_HEADS, RNN_BW, RNN_BW), RNN_BW ** -0.5),
        'rg_bx': nrm(ks[27], (N_ODD, 2, RNN_HEADS, RNN_BW), 0.01),
        'rg_lambda': jnp.log(a0) - jnp.log1p(-a0),
        'odd_w_out': nrm(ks[29], (N_ODD, D_RNN, D_MODEL), D_RNN ** -0.5),
    }


def reference(x_prompt, x_sample, ffn_norm, ffn_w_gate, ffn_w_up, ffn_w_down, even_norm, even_w_in,
              hy_short_w, hy_short_b, hy_filt_w1, hy_filt_b1, hy_filt_w2, hy_filt_b2, hy_filt_w3,
              hy_filt_freq, hy_bias, q_norm, k_norm, even_w_out, odd_norm, odd_w_in, rg_conv_w, rg_conv_b,
              rg_wa, rg_ba, rg_wx, rg_bx, rg_lambda, odd_w_out):
    p = dict(ffn_norm=ffn_norm, ffn_w_gate=ffn_w_gate, ffn_w_up=ffn_w_up, ffn_w_down=ffn_w_down,
             even_norm=even_norm, even_w_in=even_w_in, hy_short_w=hy_short_w, hy_short_b=hy_short_b,
             hy_filt_w1=hy_filt_w1, hy_filt_b1=hy_filt_b1, hy_filt_w2=hy_filt_w2, hy_filt_b2=hy_filt_b2,
             hy_filt_w3=hy_filt_w3, hy_filt_freq=hy_filt_freq, hy_bias=hy_bias, q_norm=q_norm, k_norm=k_norm,
             even_w_out=even_w_out, odd_norm=odd_norm, odd_w_in=odd_w_in, rg_conv_w=rg_conv_w,
             rg_conv_b=rg_conv_b, rg_wa=rg_wa, rg_ba=rg_ba, rg_wx=rg_wx, rg_bx=rg_bx, rg_lambda=rg_lambda,
             odd_w_out=odd_w_out)
    y_prompt = _trunk(x_prompt, p)
    y_sample = _trunk(x_sample, p)
    return (y_prompt, y_sample)
```

```python
import functools
import math

import jax
import jax.numpy as jnp
from jax import lax
from jax.experimental import pallas as pl
from jax.experimental.pallas import tpu as pltpu

F32 = jnp.float32
BF16 = jnp.bfloat16

NORM_EPS = 1e-6
GRID_W = 64
HEAD_DIM = 128
N_Q_HEADS = 8
N_KV_HEADS = 2
GQA_GROUP = N_Q_HEADS // N_KV_HEADS
ROPE_THETA = 10000.0
HY_ORDER = 2
HY_SHORT_W = 3
HY_EMB_DIM = 33
HY_BANDS = (HY_EMB_DIM - 1) // 2
HY_N_INNER = 2
HY_FAST_DECAY = 0.3
HY_SLOW_DECAY = 1.5
HY_DECAY_TARGET = 1e-2
RNN_HEADS = 8
RNN_CONV_W = 4
RNN_C = 8.0

LANES = 128
SUBLANES = 8
VMEM_LIMIT = 56 * 1024 * 1024

TM = 512
TF = 512
FB = 256
SCAN_CHUNKS = 8
SCAN_PITCH_PAD = 8


def _params(sem, vmem=VMEM_LIMIT):
    return pltpu.CompilerParams(dimension_semantics=sem, vmem_limit_bytes=vmem)


def _rms(x, g):
    ms = jnp.mean(x * x, axis=-1, keepdims=True)
    return x * lax.rsqrt(ms + NORM_EPS) * g


def _ffn_kernel(x_ref, g_ref, wg_ref, wu_ref, wd_ref, o_ref, n_ref):
    @pl.when(pl.program_id(1) == 0)
    def _():
        x = x_ref[...]
        n_ref[...] = _rms(x, g_ref[...]).astype(BF16)
        o_ref[...] = x

    n = n_ref[...]
    hg = jnp.dot(n, wg_ref[...], preferred_element_type=F32)
    hu = jnp.dot(n, wu_ref[...], preferred_element_type=F32)
    h = (hg * jax.nn.sigmoid(hg)) * (hu * 0.5)
    o_ref[...] += jnp.dot(h.astype(BF16), wd_ref[...], preferred_element_type=F32)


def _ffn_half(x, g, wg, wu, wd):
    t, d = x.shape
    fp = wg.shape[1]
    return pl.pallas_call(
        _ffn_kernel,
        grid=(t // TM, fp // TF),
        in_specs=[
            pl.BlockSpec((TM, d), lambda i, j: (i, 0)),
            pl.BlockSpec((1, d), lambda i, j: (0, 0)),
            pl.BlockSpec((d, TF), lambda i, j: (0, j)),
            pl.BlockSpec((d, TF), lambda i, j: (0, j)),
            pl.BlockSpec((TF, d), lambda i, j: (j, 0)),
        ],
        out_specs=pl.BlockSpec((TM, d), lambda i, j: (i, 0)),
        out_shape=jax.ShapeDtypeStruct((t, d), F32),
        scratch_shapes=[pltpu.VMEM((TM, d), BF16)],
        compiler_params=_params(("parallel", "arbitrary")),
        name="ffn_half",
    )(x, g, wg, wu, wd)


def _proj_kernel(x_ref, g_ref, w_ref, o_ref, n_ref):
    @pl.when(pl.program_id(1) == 0)
    def _():
        n_ref[...] = _rms(x_ref[...], g_ref[...]).astype(BF16)

    o_ref[...] = jnp.dot(n_ref[...], w_ref[...], preferred_element_type=F32).astype(o_ref.dtype)


def _norm_proj(x, g, w, tn, out_dtype, name):
    t, d = x.shape
    n = w.shape[1]
    return pl.pallas_call(
        _proj_kernel,
        grid=(t // TM, n // tn),
        in_specs=[
            pl.BlockSpec((TM, d), lambda i, j: (i, 0)),
            pl.BlockSpec((1, d), lambda i, j: (0, 0)),
            pl.BlockSpec((d, tn), lambda i, j: (0, j)),
        ],
        out_specs=pl.BlockSpec((TM, tn), lambda i, j: (i, j)),
        out_shape=jax.ShapeDtypeStruct((t, n), out_dtype),
        scratch_shapes=[pltpu.VMEM((TM, d), BF16)],
        compiler_params=_params(("parallel", "arbitrary")),
        name=name,
    )(x, g, w)


def _qkv_kernel(x_ref, g_ref, w_ref, hg_ref, cos_ref, sin_ref, o_ref, *, n_rope):
    n = _rms(x_ref[...], g_ref[...]).astype(BF16)
    u = jnp.dot(n, w_ref[...], preferred_element_type=F32)
    c = cos_ref[...]
    s = sin_ref[...]
    for h in range(u.shape[1] // HEAD_DIM):
        sl = slice(h * HEAD_DIM, (h + 1) * HEAD_DIM)
        uh = u[:, sl]
        if h < n_rope:
            ms = jnp.mean(uh * uh, axis=-1, keepdims=True)
            uh = uh * lax.rsqrt(ms + NORM_EPS) * hg_ref[:, sl]
            uh = uh * c + pltpu.roll(uh, HEAD_DIM // 2, 1) * s
        o_ref[:, sl] = uh.astype(o_ref.dtype)


def _qkv_proj(x, g, w, head_gain, cos, sin, seq_len):
    t, d = x.shape
    n = w.shape[1]
    tiles_per_seq = seq_len // TM
    n_rope = N_Q_HEADS + N_KV_HEADS
    return pl.pallas_call(
        functools.partial(_qkv_kernel, n_rope=n_rope),
        grid=(t // TM,),
        in_specs=[
            pl.BlockSpec((TM, d), lambda i: (i, 0)),
            pl.BlockSpec((1, d), lambda i: (0, 0)),
            pl.BlockSpec((d, n), lambda i: (0, 0)),
            pl.BlockSpec((1, n), lambda i: (0, 0)),
            pl.BlockSpec((TM, HEAD_DIM), lambda i: (i % tiles_per_seq, 0)),
            pl.BlockSpec((TM, HEAD_DIM), lambda i: (i % tiles_per_seq, 0)),
        ],
        out_specs=pl.BlockSpec((TM, n), lambda i: (i, 0)),
        out_shape=jax.ShapeDtypeStruct((t, n), BF16),
        compiler_params=_params(("parallel",)),
        name="qkv_proj",
    )(x, g, w, head_gain, cos, sin)


def _out1_kernel(r_ref, y_ref, w_ref, o_ref):
    o_ref[...] = r_ref[...] + jnp.dot(y_ref[...], w_ref[...], preferred_element_type=F32)


def _out_proj(res, y, w, tn, name):
    t, d = res.shape
    k = y.shape[1]
    return pl.pallas_call(
        _out1_kernel,
        grid=(t // TM, d // tn),
        in_specs=[
            pl.BlockSpec((TM, tn), lambda i, j: (i, j)),
            pl.BlockSpec((TM, k), lambda i, j: (i, 0)),
            pl.BlockSpec((k, tn), lambda i, j: (0, j)),
        ],
        out_specs=pl.BlockSpec((TM, tn), lambda i, j: (i, j)),
        out_shape=jax.ShapeDtypeStruct((t, d), F32),
        compiler_params=_params(("parallel", "parallel")),
        name=name,
    )(res, y, w)


def _out2_kernel(r_ref, ya_ref, yb_ref, wa_ref, wb_ref, o_ref):
    acc = jnp.dot(ya_ref[...], wa_ref[...], preferred_element_type=F32)
    acc += jnp.dot(yb_ref[...], wb_ref[...], preferred_element_type=F32)
    o_ref[...] = r_ref[...] + acc


def _out_proj2(res, ya, yb, w, tn, name):
    t, d = res.shape
    ka = ya.shape[1]
    kb = yb.shape[1]
    return pl.pallas_call(
        _out2_kernel,
        grid=(t // TM, d // tn),
        in_specs=[
            pl.BlockSpec((TM, tn), lambda i, j: (i, j)),
            pl.BlockSpec((TM, ka), lambda i, j: (i, 0)),
            pl.BlockSpec((TM, kb), lambda i, j: (i, 0)),
            pl.BlockSpec((ka, tn), lambda i, j: (0, j)),
            pl.BlockSpec((kb, tn), lambda i, j: (ka // kb, j)),
        ],
        out_specs=pl.BlockSpec((TM, tn), lambda i, j: (i, j)),
        out_shape=jax.ShapeDtypeStruct((t, d), F32),
        compiler_params=_params(("parallel", "parallel")),
        name=name,
    )(res, ya, yb, w, w)


def _attn_kernel(q_ref, k_ref, v_ref, o_ref):
    k = k_ref[...]
    v = v_ref[...]
    for g in range(q_ref.shape[1] // HEAD_DIM):
        sl = slice(g * HEAD_DIM, (g + 1) * HEAD_DIM)
        s = lax.dot_general(q_ref[:, sl], k, (((1,), (1,)), ((), ())), preferred_element_type=F32)
        m = jnp.max(s, axis=-1, keepdims=True)
        p = jnp.exp(s - m)
        l = jnp.sum(p, axis=-1, keepdims=True)
        o = jnp.dot(p.astype(BF16), v, preferred_element_type=F32)
        o_ref[:, sl] = (o / l).astype(o_ref.dtype)


def _attention(qkv, batch, seq_len, tq):
    t = qkv.shape[0]
    gw = GQA_GROUP * HEAD_DIM
    nq = seq_len // tq
    k_off = N_Q_HEADS
    v_off = N_Q_HEADS + N_KV_HEADS
    return pl.pallas_call(
        _attn_kernel,
        grid=(batch, N_KV_HEADS, nq),
        in_specs=[
            pl.BlockSpec((tq, gw), lambda b, h, i: (b * nq + i, h)),
            pl.BlockSpec((seq_len, HEAD_DIM), lambda b, h, i: (b, k_off + h)),
            pl.BlockSpec((seq_len, HEAD_DIM), lambda b, h, i: (b, v_off + h)),
        ],
        out_specs=pl.BlockSpec((tq, gw), lambda b, h, i: (b * nq + i, h)),
        out_shape=jax.ShapeDtypeStruct((t, N_Q_HEADS * HEAD_DIM), BF16),
        compiler_params=_params(("parallel", "parallel", "arbitrary")),
        name="attention",
    )(qkv, qkv, qkv)


def _shortconv_kernel(u_ref, w_ref, b_ref, o_ref, pad_ref, *, rows):
    seq = u_ref.shape[0]
    zeros = jnp.zeros((SUBLANES, pad_ref.shape[1]), F32)
    pad_ref[0:SUBLANES, :] = zeros
    pad_ref[seq + SUBLANES:seq + 2 * SUBLANES, :] = zeros
    pad_ref[SUBLANES:seq + SUBLANES, :] = u_ref[...]
    w = w_ref[...]
    b = b_ref[...]

    def body(i, carry):
        r0 = pl.multiple_of(i * rows, rows)
        win = pad_ref[pl.ds(r0, rows + 2 * SUBLANES), :]
        y = b
        for j in range(HY_SHORT_W):
            lo = SUBLANES - HY_SHORT_W // 2 + j
            y = y + win[lo:lo + rows, :] * w[j:j + 1, :]
        o_ref[pl.ds(r0, rows), :] = y
        return carry

    lax.fori_loop(0, seq // rows, body, 0)


def _short_conv(u, w, b, batch, seq_len, tc):
    t, c = u.shape
    return pl.pallas_call(
        functools.partial(_shortconv_kernel, rows=64),
        grid=(batch, c // tc),
        in_specs=[
            pl.BlockSpec((seq_len, tc), lambda bi, j: (bi, j)),
            pl.BlockSpec((HY_SHORT_W, tc), lambda bi, j: (0, j)),
            pl.BlockSpec((1, tc), lambda bi, j: (0, j)),
        ],
        out_specs=pl.BlockSpec((seq_len, tc), lambda bi, j: (bi, j)),
        out_shape=jax.ShapeDtypeStruct((t, c), F32),
        scratch_shapes=[pltpu.VMEM((seq_len + 2 * SUBLANES, tc), F32)],
        compiler_params=_params(("parallel", "parallel")),
        name="hyena_short_conv",
    )(u, w, b)


def _filter_kernel(z_ref, w1_ref, b1_ref, w2_ref, b2_ref, w3_ref, fr_ref, dl_ref, ks_ref, kd_ref, *, seq_len):
    hi = lax.Precision.HIGHEST
    fr = fr_ref[...]
    h = jnp.sin(fr * (jnp.dot(z_ref[...], w1_ref[...], precision=hi, preferred_element_type=F32) + b1_ref[...]))
    for i in range(HY_N_INNER):
        h = jnp.sin(fr * (jnp.dot(h, w2_ref[i], precision=hi, preferred_element_type=F32) + b2_ref[i]))
    k = jnp.dot(h, w3_ref[...], precision=hi, preferred_element_type=F32)
    tl, half = ks_ref.shape
    pos = lax.broadcasted_iota(jnp.int32, (tl, half), 0) + pl.program_id(0) * tl
    tnorm = pos.astype(F32) / (seq_len - 1)
    decay = jnp.exp(-tnorm * dl_ref[...])
    kf = k[:, :half] * decay
    kb = jnp.where(pos == 0, 0.0, k[:, half:] * decay)
    ks_ref[...] = (kf + kb).astype(ks_ref.dtype)
    kd_ref[...] = (kf - kb).astype(kd_ref.dtype)


def _hyena_filter_taps(zfeat, w1, b1, w2, b2, w3, freq, deltas2, seq_len, tl):
    half = deltas2.shape[1]
    hid = w1.shape[1]
    return pl.pallas_call(
        functools.partial(_filter_kernel, seq_len=seq_len),
        grid=(seq_len // tl,),
        in_specs=[
            pl.BlockSpec((tl, zfeat.shape[1]), lambda i: (i, 0)),
            pl.BlockSpec(w1.shape, lambda i: (0, 0)),
            pl.BlockSpec((1, hid), lambda i: (0, 0)),
            pl.BlockSpec(w2.shape, lambda i: (0, 0, 0)),
            pl.BlockSpec(b2.shape, lambda i: (0, 0, 0)),
            pl.BlockSpec(w3.shape, lambda i: (0, 0)),
            pl.BlockSpec((1, hid), lambda i: (0, 0)),
            pl.BlockSpec((1, half), lambda i: (0, 0)),
        ],
        out_specs=[pl.BlockSpec((tl, half), lambda i: (i, 0))] * 2,
        out_shape=[jax.ShapeDtypeStruct((seq_len, half), BF16)] * 2,
        compiler_params=_params(("parallel",)),
        name="hyena_filter_taps",
    )(zfeat, w1, b1, w2, b2, w3, freq, deltas2)


def _spectrum_kernel(p_ref, ks_ref, kd_ref, ka_ref, kc_ref, ny_ref, *, period):
    fb = ka_ref.shape[0]
    ks = ks_ref[...]
    re = jnp.dot(p_ref[0:fb, :], ks, preferred_element_type=F32)
    im = jnp.dot(p_ref[fb:2 * fb, :], kd_ref[...], preferred_element_type=F32)
    ny = jnp.dot(p_ref[fb:fb + SUBLANES, :], ks, preferred_element_type=F32)
    is_dc = (lax.broadcasted_iota(jnp.int32, re.shape, 0) + pl.program_id(1) * fb) == 0
    ka_ref[...] = re * jnp.where(is_dc, 1.0 / period, 2.0 / period)
    kc_ref[...] = jnp.where(is_dc, 0.0, im * (2.0 / period))
    ny_ref[...] = ny * (1.0 / period)


def _hyena_spectrum(p, ksum, kdiff, tcs):
    seq_len, cols = ksum.shape
    nf = p.shape[0] // (2 * FB)
    return pl.pallas_call(
        functools.partial(_spectrum_kernel, period=2 * seq_len),
        grid=(cols // tcs, nf),
        in_specs=[
            pl.BlockSpec((2 * FB, seq_len), lambda c, f: (f, 0)),
            pl.BlockSpec((seq_len, tcs), lambda c, f: (0, c)),
            pl.BlockSpec((seq_len, tcs), lambda c, f: (0, c)),
        ],
        out_specs=[
            pl.BlockSpec((FB, tcs), lambda c, f: (f, c)),
            pl.BlockSpec((FB, tcs), lambda c, f: (f, c)),
            pl.BlockSpec((SUBLANES, tcs), lambda c, f: (f, c)),
        ],
        out_shape=[
            jax.ShapeDtypeStruct((nf * FB, cols), F32),
            jax.ShapeDtypeStruct((nf * FB, cols), F32),
            jax.ShapeDtypeStruct((nf * SUBLANES, cols), F32),
        ],
        compiler_params=_params(("parallel", "parallel")),
        name="hyena_filter_spectrum",
    )(p, ksum, kdiff)


def _longconv_kernel(z_ref, gate_ref, p_ref, ka_ref, kc_ref, ny_ref, bias_ref, o_ref, zb_ref, acc_ref):
    f = pl.program_id(2)
    fb = ka_ref.shape[0]

    @pl.when(f == 0)
    def _():
        zb_ref[...] = z_ref[...].astype(BF16)
        acc_ref[...] = jnp.zeros_like(acc_ref)

    spec = jnp.dot(p_ref[...], zb_ref[...], preferred_element_type=F32)
    xre = spec[:fb]
    xim = spec[fb:]
    ka = ka_ref[...]
    kc = kc_ref[...]
    is_dc = (lax.broadcasted_iota(jnp.int32, ka.shape, 0) + f * fb) == 0
    kd = jnp.where(is_dc, ny_ref[0:1, :], ka)
    yre = (xre * ka - xim * kc).astype(BF16)
    yim = (xre * kc + xim * kd).astype(BF16)
    tn = (((0,), (0,)), ((), ()))
    acc_ref[...] += (lax.dot_general(p_ref[0:fb, :], yre, tn, preferred_element_type=F32)
                     + lax.dot_general(p_ref[fb:2 * fb, :], yim, tn, preferred_element_type=F32))

    @pl.when(f == pl.num_programs(2) - 1)
    def _():
        o_ref[...] = (gate_ref[...] * (acc_ref[...] + z_ref[...] * bias_ref[...])).astype(o_ref.dtype)


def _long_conv(z, z_col, gate, gate_col, p, ka, kc, ny, bias, order, batch, seq_len, tc, out_dtype):
    c = bias.shape[1] // HY_ORDER
    t = z.shape[0]
    nc = c // tc
    nf = p.shape[0] // (2 * FB)
    return pl.pallas_call(
        _longconv_kernel,
        grid=(batch, nc, nf),
        in_specs=[
            pl.BlockSpec((seq_len, tc), lambda b, j, f: (b, z_col * nc + j)),
            pl.BlockSpec((seq_len, tc), lambda b, j, f: (b, gate_col * nc + j)),
            pl.BlockSpec((2 * FB, seq_len), lambda b, j, f: (f, 0)),
            pl.BlockSpec((FB, tc), lambda b, j, f: (f, order * nc + j)),
            pl.BlockSpec((FB, tc), lambda b, j, f: (f, order * nc + j)),
            pl.BlockSpec((SUBLANES, tc), lambda b, j, f: (0, order * nc + j)),
            pl.BlockSpec((1, tc), lambda b, j, f: (0, order * nc + j)),
        ],
        out_specs=pl.BlockSpec((seq_len, tc), lambda b, j, f: (b, j)),
        out_shape=jax.ShapeDtypeStruct((t, c), out_dtype),
        scratch_shapes=[pltpu.VMEM((seq_len, tc), BF16), pltpu.VMEM((seq_len, tc), F32)],
        compiler_params=_params(("parallel", "parallel", "arbitrary")),
        name="hyena_long_conv",
    )(z, gate, p, ka, kc, ny, bias)


def _dft_matrix(seq_len):
    period = 2 * seq_len
    split = 64
    f = jnp.arange(seq_len, dtype=jnp.int32)[:, None]

    def trig(tvals):
        ang = ((f * tvals[None, :]) % period).astype(F32) * (2.0 * math.pi / period)
        return jnp.cos(ang), jnp.sin(ang)

    ca, sa = trig(jnp.arange(seq_len // split, dtype=jnp.int32) * split)
    cb, sb = trig(jnp.arange(split, dtype=jnp.int32))
    re = (ca[:, :, None] * cb[:, None, :] - sa[:, :, None] * sb[:, None, :]).reshape(seq_len, seq_len)
    im = -(sa[:, :, None] * cb[:, None, :] + ca[:, :, None] * sb[:, None, :]).reshape(seq_len, seq_len)
    nyq = jnp.where(jnp.arange(seq_len) % 2 == 0, 1.0, -1.0).astype(F32)
    im = im.at[0].set(nyq)
    nf = seq_len // FB
    p = jnp.stack([re.reshape(nf, FB, seq_len), im.reshape(nf, FB, seq_len)], axis=1)
    return p.reshape(2 * seq_len, seq_len).astype(BF16)


def _position_features(seq_len, width):
    pos = jnp.arange(seq_len, dtype=F32)
    t = pos / (seq_len - 1)
    w = 2.0 * math.pi * pos / seq_len
    f = jnp.linspace(1e-4, HY_BANDS - 1, HY_BANDS, dtype=F32)
    fw = w[:, None] * f[None, :]
    z = jnp.concatenate([t[:, None], jnp.cos(fw), -jnp.sin(fw)], axis=-1)
    return jnp.pad(z, ((0, 0), (0, width - z.shape[1])))


def _softplus(x):
    return jnp.maximum(x, 0.0) + jnp.log1p(jnp.exp(-jnp.abs(x)))


def _gelu_tanh(x):
    return 0.5 * x * (1.0 + jnp.tanh(math.sqrt(2.0 / math.pi) * (x + 0.044715 * (x * x * x))))


def _rglru_kernel(g_ref, x_ref, cw_ref, cb_ref, w_ref, bias_ref, lam_ref, o_ref,
                  pad_ref, a_ref, b_ref, hl_ref, al_ref, hs_ref):
    seq, width = x_ref.shape
    chunk = seq // SCAN_CHUNKS
    pitch = chunk + SCAN_PITCH_PAD
    slabs = width // LANES
    halo = SUBLANES
    zeros = jnp.zeros((halo, width), F32)
    pad_ref[0:halo, :] = zeros
    pad_ref[seq + halo:seq + 2 * halo, :] = zeros
    pad_ref[halo:seq + halo, :] = x_ref[...]

    for d in range(2):
        sp = _softplus(-lam_ref[d:d + 1, :])
        start_t = 0 if d == 0 else seq - 1

        def gates(c, carry):
            r0 = pl.multiple_of(c * chunk, chunk)
            win = pad_ref[pl.ds(r0, chunk + 2 * halo), :]
            xc = cb_ref[d:d + 1, :]
            for j in range(RNN_CONV_W):
                off = (j - (RNN_CONV_W - 1)) if d == 0 else (RNN_CONV_W - 1 - j)
                xc = xc + win[halo + off:halo + off + chunk, :] * cw_ref[d, j:j + 1, :]
            gt = jnp.dot(xc.astype(BF16), w_ref[d, 0], preferred_element_type=F32) + bias_ref[d, 0]
            r = jax.nn.sigmoid(gt[:, :width])
            i = jax.nn.sigmoid(gt[:, width:])
            log_a = (-RNN_C) * r * sp
            a = jnp.exp(log_a)
            mult = jnp.sqrt(-jnp.tanh(log_a) * (a * a + 1.0))
            tpos = lax.broadcasted_iota(jnp.int32, (chunk, width), 0) + r0
            mult = jnp.where(tpos == start_t, 1.0, mult)
            bt = mult * (i * xc)
            p0 = pl.multiple_of(c * pitch, SUBLANES)
            for s in range(slabs):
                a_ref[s, pl.ds(p0, chunk), :] = a[:, s * LANES:(s + 1) * LANES]
                b_ref[s, pl.ds(p0, chunk), :] = bt[:, s * LANES:(s + 1) * LANES]
            return carry

        lax.fori_loop(0, SCAN_CHUNKS, gates, 0)

        def local_scan(i, carry):
            t = i if d == 0 else chunk - 1 - i
            out = []
            for s in range(slabs):
                h, acc = carry[2 * s], carry[2 * s + 1]
                av = a_ref[s, pl.ds(t, SCAN_CHUNKS, stride=pitch), :]
                bv = b_ref[s, pl.ds(t, SCAN_CHUNKS, stride=pitch), :]
                h = av * h + bv
                acc = acc * av
                row = pl.multiple_of(t * SCAN_CHUNKS, SCAN_CHUNKS)
                hl_ref[s, pl.ds(row, SCAN_CHUNKS), :] = h
                al_ref[s, pl.ds(row, SCAN_CHUNKS), :] = acc
                out += [h, acc]
            return tuple(out)

        init = (jnp.zeros((SCAN_CHUNKS, LANES), F32), jnp.ones((SCAN_CHUNKS, LANES), F32)) * slabs
        ends = lax.fori_loop(0, chunk, local_scan, init)

        carries = []
        for s in range(slabs):
            h_end, a_end = ends[2 * s], ends[2 * s + 1]
            rows = [None] * SCAN_CHUNKS
            cur = jnp.zeros((1, LANES), F32)
            order = range(SCAN_CHUNKS) if d == 0 else range(SCAN_CHUNKS - 1, -1, -1)
            for c in order:
                rows[c] = cur
                cur = a_end[c:c + 1, :] * cur + h_end[c:c + 1, :]
            carries.append(jnp.concatenate(rows, axis=0))

        def fold(t, carry):
            row = pl.multiple_of(t * SCAN_CHUNKS, SCAN_CHUNKS)
            for s in range(slabs):
                hv = hl_ref[s, pl.ds(row, SCAN_CHUNKS), :] + al_ref[s, pl.ds(row, SCAN_CHUNKS), :] * carries[s]
                hs_ref[d, s, pl.ds(t, SCAN_CHUNKS, stride=pitch), :] = hv
            return carry

        lax.fori_loop(0, chunk, fold, 0)

    def finish(c, carry):
        r0 = pl.multiple_of(c * chunk, chunk)
        p0 = pl.multiple_of(c * pitch, SUBLANES)
        gate = _gelu_tanh(g_ref[pl.ds(r0, chunk), :])
        for s in range(slabs):
            hsum = hs_ref[0, s, pl.ds(p0, chunk), :] + hs_ref[1, s, pl.ds(p0, chunk), :]
            o_ref[pl.ds(r0, chunk), s * LANES:(s + 1) * LANES] = (
                hsum * gate[:, s * LANES:(s + 1) * LANES]).astype(o_ref.dtype)
        return carry

    lax.fori_loop(0, SCAN_CHUNKS, finish, 0)


def _rglru(u, conv_w, conv_b, w_gates, b_gates, lam, batch, seq_len):
    t, two_d = u.shape
    d = two_d // 2
    width = d // RNN_HEADS
    slabs = width // LANES
    pitch = seq_len // SCAN_CHUNKS + SCAN_PITCH_PAD
    scan_rows = SCAN_CHUNKS * pitch
    return pl.pallas_call(
        _rglru_kernel,
        grid=(batch, RNN_HEADS),
        in_specs=[
            pl.BlockSpec((seq_len, width), lambda b, h: (b, h)),
            pl.BlockSpec((seq_len, width), lambda b, h: (b, RNN_HEADS + h)),
            pl.BlockSpec((2, RNN_CONV_W, width), lambda b, h: (0, 0, h)),
            pl.BlockSpec((2, width), lambda b, h: (0, h)),
            pl.BlockSpec((2, 1, width, 2 * width), lambda b, h: (0, h, 0, 0)),
            pl.BlockSpec((2, 1, 1, 2 * width), lambda b, h: (0, h, 0, 0)),
            pl.BlockSpec((2, width), lambda b, h: (0, h)),
        ],
        out_specs=pl.BlockSpec((seq_len, width), lambda b, h: (b, h)),
        out_shape=jax.ShapeDtypeStruct((t, d), BF16),
        scratch_shapes=[
            pltpu.VMEM((seq_len + 2 * SUBLANES, width), F32),
            pltpu.VMEM((slabs, scan_rows, LANES), F32),
            pltpu.VMEM((slabs, scan_rows, LANES), F32),
            pltpu.VMEM((slabs, seq_len, LANES), F32),
            pltpu.VMEM((slabs, seq_len, LANES), F32),
            pltpu.VMEM((2, slabs, scan_rows, LANES), F32),
        ],
        compiler_params=_params(("parallel", "parallel")),
        name="rglru",
    )(u, u, conv_w, conv_b, w_gates, b_gates, lam)


def _pad_cols(w, mult):
    pad = (-w.shape[-1]) % mult
    return jnp.pad(w, ((0, 0), (0, pad))) if pad else w


def _deinterleave_heads(w, n_heads):
    lead = w.shape[:-1]
    w = w.reshape(lead + (n_heads, HEAD_DIM // 2, 2))
    w = jnp.swapaxes(w, -1, -2)
    return w.reshape(lead + (n_heads * HEAD_DIM,))


def _rope_tables(seq_len):
    rows = seq_len // GRID_W
    row = jnp.repeat(jnp.arange(rows, dtype=F32), GRID_W)
    col = jnp.tile(jnp.arange(GRID_W, dtype=F32), rows)
    axis_dim = HEAD_DIM // 2
    omega = ROPE_THETA ** (-jnp.arange(0, axis_dim, 2, dtype=F32) / axis_dim)
    ang = jnp.concatenate([row[:, None] * omega[None], col[:, None] * omega[None]], axis=-1)
    c, s = jnp.cos(ang), jnp.sin(ang)
    return jnp.concatenate([c, c], axis=-1), jnp.concatenate([-s, s], axis=-1)


def _even_mixer(x, batch, seq_len, norm_g, w_in, short_w, short_b, f_w1, f_b1, f_w2, f_b2, f_w3, f_freq,
                hy_bias, q_g, k_g, w_out):
    d = x.shape[1]
    hy_d = hy_bias.shape[1]
    s0 = (HY_ORDER + 1) * hy_d
    nq = N_Q_HEADS * HEAD_DIM
    nk = N_KV_HEADS * HEAD_DIM
    g = norm_g.reshape(1, d)

    u_hy = _norm_proj(x, g, w_in[:, :s0].astype(BF16), 1024, F32, "hyena_in_proj")
    u_hy = _short_conv(u_hy, short_w, short_b.reshape(1, s0), batch, seq_len, 512)

    hid = LANES
    hpad = hid - f_w1.shape[1]
    zfeat = _position_features(seq_len, LANES)
    w1 = jnp.pad(f_w1, ((0, LANES - f_w1.shape[0]), (0, hpad)))
    b1 = jnp.pad(f_b1, (0, hpad)).reshape(1, hid)
    w2 = jnp.pad(f_w2, ((0, 0), (0, hpad), (0, hpad)))
    b2 = jnp.pad(f_b2, ((0, 0), (0, hpad))).reshape(HY_N_INNER, 1, hid)
    w3 = jnp.pad(f_w3, ((0, hpad), (0, 0)))
    freq = jnp.pad(f_freq, (0, hpad)).reshape(1, hid)
    max_decay = math.log(HY_DECAY_TARGET) / HY_FAST_DECAY
    min_decay = math.log(HY_DECAY_TARGET) / HY_SLOW_DECAY
    deltas = jnp.abs(jnp.linspace(min_decay, max_decay, hy_d, dtype=F32))
    deltas2 = jnp.tile(deltas, HY_ORDER).reshape(1, HY_ORDER * hy_d)
    ksum, kdiff = _hyena_filter_taps(zfeat, w1, b1, w2, b2, w3, freq, deltas2, seq_len, 256)
    p = _dft_matrix(seq_len)
    ka, kc, ny = _hyena_spectrum(p, ksum, kdiff, 512)
    bias = hy_bias.reshape(1, HY_ORDER * hy_d)
    z1 = _long_conv(u_hy, 0, u_hy, 1, p, ka, kc, ny, bias, 0, batch, seq_len, 512, F32)
    y_hy = _long_conv(z1, 0, u_hy, 2, p, ka, kc, ny, bias, 1, batch, seq_len, 512, BF16)

    w_qkv = jnp.concatenate([
        _deinterleave_heads(w_in[:, s0:s0 + nq], N_Q_HEADS),
        _deinterleave_heads(w_in[:, s0 + nq:s0 + nq + nk], N_KV_HEADS),
        w_in[:, s0 + nq + nk:]], axis=1).astype(BF16)
    qg = _deinterleave_heads(q_g, 1) * (HEAD_DIM ** -0.5)
    kg = _deinterleave_heads(k_g, 1)
    head_gain = jnp.concatenate([jnp.tile(qg, N_Q_HEADS), jnp.tile(kg, N_KV_HEADS),
                                 jnp.ones((nk,), F32)]).reshape(1, nq + 2 * nk)
    cos, sin = _rope_tables(seq_len)
    qkv = _qkv_proj(x, g, w_qkv, head_gain, cos, sin, seq_len)
    y_at = _attention(qkv, batch, seq_len, 512)

    return _out_proj2(x, y_hy, y_at, w_out.astype(BF16), 1024, "even_out_proj")


def _odd_mixer(x, batch, seq_len, norm_g, w_in, conv_w, conv_b, wa, ba, wx, bx, lam, w_out):
    d = x.shape[1]
    u = _norm_proj(x, norm_g.reshape(1, d), w_in.astype(BF16), 1024, F32, "rglru_in_proj")
    w_gates = jnp.concatenate([wa, wx], axis=-1).astype(BF16)
    b_gates = jnp.concatenate([ba, bx], axis=-1)[:, :, None, :]
    y = _rglru(u, conv_w, conv_b, w_gates, b_gates, lam, batch, seq_len)
    return _out_proj(x, y, w_out.astype(BF16), 1024, "rglru_out_proj")


def kernel(x_prompt, x_sample, ffn_norm, ffn_w_gate, ffn_w_up, ffn_w_down, even_norm, even_w_in, hy_short_w, hy_short_b, hy_filt_w1, hy_filt_b1, hy_filt_w2, hy_filt_b2, hy_filt_w3, hy_filt_freq, hy_bias, q_norm, k_norm, even_w_out, odd_norm, odd_w_in, rg_conv_w, rg_conv_b, rg_wa, rg_ba, rg_wx, rg_bx, rg_lambda, odd_w_out):
    bp, seq_len, d = x_prompt.shape
    bs = x_sample.shape[0]
    assert x_sample.shape[1:] == (seq_len, d)
    batch = bp + bs
    x = jnp.concatenate([x_prompt.reshape(bp * seq_len, d), x_sample.reshape(bs * seq_len, d)], axis=0)

    depth = ffn_norm.shape[0]
    wg = _pad_cols(ffn_w_gate.astype(BF16).reshape(-1, ffn_w_gate.shape[-1]), TF).reshape(depth, 2, d, -1)
    wu = _pad_cols(ffn_w_up.astype(BF16).reshape(-1, ffn_w_up.shape[-1]), TF).reshape(depth, 2, d, -1)
    fp = wg.shape[-1]
    wd = jnp.pad(ffn_w_down.astype(BF16), ((0, 0), (0, 0), (0, fp - ffn_w_down.shape[2]), (0, 0)))

    for layer in range(depth):
        x = _ffn_half(x, ffn_norm[layer, 0].reshape(1, d), wg[layer, 0], wu[layer, 0], wd[layer, 0])
        j = layer // 2
        if layer % 2 == 0:
            x = _even_mixer(x, batch, seq_len, even_norm[j], even_w_in[j], hy_short_w[j], hy_short_b[j],
                            hy_filt_w1[j], hy_filt_b1[j], hy_filt_w2[j], hy_filt_b2[j], hy_filt_w3[j],
                            hy_filt_freq[j], hy_bias[j], q_norm[j], k_norm[j], even_w_out[j])
        else:
            x = _odd_mixer(x, batch, seq_len, odd_norm[j], odd_w_in[j], rg_conv_w[j], rg_conv_b[j],
                           rg_wa[j], rg_ba[j], rg_wx[j], rg_bx[j], rg_lambda[j], odd_w_out[j])
        x = _ffn_half(x, ffn_norm[layer, 1].reshape(1, d), wg[layer, 1], wu[layer, 1], wd[layer, 1])

    y_prompt = x[:bp * seq_len].reshape(bp, seq_len, d)
    y_sample = x[bp * seq_len:].reshape(bs, seq_len, d)
    return (y_prompt, y_sample)
```

```python
import functools
import math

import jax
import jax.numpy as jnp
from jax import lax
from jax.experimental import pallas as pl
from jax.experimental.pallas import tpu as pltpu

F32 = jnp.float32
BF16 = jnp.bfloat16

NORM_EPS = 1e-6
GRID_W = 64
HEAD_DIM = 128
N_Q_HEADS = 8
N_KV_HEADS = 2
GQA_GROUP = N_Q_HEADS // N_KV_HEADS
ROPE_THETA = 10000.0
HY_ORDER = 2
HY_SHORT_W = 3
HY_EMB_DIM = 33
HY_BANDS = (HY_EMB_DIM - 1) // 2
HY_N_INNER = 2
HY_FAST_DECAY = 0.3
HY_SLOW_DECAY = 1.5
HY_DECAY_TARGET = 1e-2
RNN_HEADS = 8
RNN_CONV_W = 4
RNN_C = 8.0
LOG2E = 1.4426950408889634

LANES = 128
SUBLANES = 8
VMEM_LIMIT = 56 * 1024 * 1024

TM = 512
TM_FFN = 1024
TF = 512
TN = 1024
FB = 256
CONV_ROWS = 64
GATE_ROWS = 256
SCAN_GROUP = 8
SCAN_PITCH_PAD = 8


def _params(sem, vmem=VMEM_LIMIT):
    return pltpu.CompilerParams(dimension_semantics=sem, vmem_limit_bytes=vmem)


def _resident(shape):
    return pl.BlockSpec(shape, lambda *_: (0,) * len(shape), pipeline_mode=pl.Buffered(1))


def _rms(x, g):
    ms = jnp.mean(x * x, axis=-1, keepdims=True)
    return x * lax.rsqrt(ms + NORM_EPS) * g


def _cast_pad_kernel(w_ref, o_ref, *, rows, cols):
    br, bc = o_ref.shape
    r = lax.broadcasted_iota(jnp.int32, (br, bc), 0) + pl.program_id(1) * br
    c = lax.broadcasted_iota(jnp.int32, (br, bc), 1) + pl.program_id(2) * bc
    w = jnp.where(r < rows, jnp.where(c < cols, w_ref[...], 0.0), 0.0)
    o_ref[...] = w.astype(o_ref.dtype)


def _cast_pad(w, rows_out, cols_out, br, bc):
    n, rows, cols = w.shape
    return pl.pallas_call(
        functools.partial(_cast_pad_kernel, rows=rows, cols=cols),
        grid=(n, rows_out // br, cols_out // bc),
        in_specs=[pl.BlockSpec((None, br, bc), lambda k, i, j: (k, i, j))],
        out_specs=pl.BlockSpec((None, br, bc), lambda k, i, j: (k, i, j)),
        out_shape=jax.ShapeDtypeStruct((n, rows_out, cols_out), BF16),
        compiler_params=_params(("parallel", "parallel", "parallel")),
        name="cast_pad",
    )(w)


def _ffn_kernel(*refs, n_in, n_out, split):
    x_refs = refs[:n_in]
    g_ref, wg_ref, wu_ref, wd_ref = refs[n_in:n_in + 4]
    o_refs = refs[n_in + 4:n_in + 4 + n_out]
    n_ref = refs[-1]
    i = pl.program_id(0)
    j = pl.program_id(1)
    in_part = [i < split, i >= split]

    for a, x_ref in enumerate(x_refs):
        for b, o_ref in enumerate(o_refs):
            cond = j == 0
            if n_in == 2:
                cond = jnp.logical_and(cond, in_part[a])
            if n_out == 2:
                cond = jnp.logical_and(cond, in_part[b])

            @pl.when(cond)
            def _(x_ref=x_ref, o_ref=o_ref):
                x = x_ref[...]
                n_ref[...] = _rms(x, g_ref[...]).astype(BF16)
                o_ref[...] = x

    n = n_ref[...]
    hg = jnp.dot(n, wg_ref[...], preferred_element_type=F32)
    hu = jnp.dot(n, wu_ref[...], preferred_element_type=F32)
    h = ((hg * jax.nn.sigmoid(hg)) * (hu * 0.5)).astype(BF16)
    contrib = jnp.dot(h, wd_ref[...], preferred_element_type=F32)
    if n_out == 1:
        o_refs[0][...] += contrib
    else:
        for b, o_ref in enumerate(o_refs):
            @pl.when(in_part[b])
            def _(o_ref=o_ref):
                o_ref[...] += contrib


def _ffn_half(xs, g, wg, wu, wd, tm):
    d = xs[0].shape[1]
    fp = wg.shape[1]
    rows = [x.shape[0] for x in xs]
    n_in = len(xs)
    split = rows[0] // tm if n_in == 2 else 0
    total = sum(rows)

    def part_spec(k):
        if k == 0:
            return pl.BlockSpec((tm, d), lambda i, j: (jnp.minimum(i, split - 1), 0))
        return pl.BlockSpec((tm, d), lambda i, j: (jnp.maximum(i - split, 0), 0))

    whole = pl.BlockSpec((tm, d), lambda i, j: (i, 0))
    in_specs = [part_spec(0), part_spec(1)] if n_in == 2 else [whole]
    return pl.pallas_call(
        functools.partial(_ffn_kernel, n_in=n_in, n_out=1, split=split),
        grid=(total // tm, fp // TF),
        in_specs=in_specs + [
            pl.BlockSpec((1, d), lambda i, j: (0, 0)),
            pl.BlockSpec((d, TF), lambda i, j: (0, j)),
            pl.BlockSpec((d, TF), lambda i, j: (0, j)),
            pl.BlockSpec((TF, d), lambda i, j: (j, 0)),
        ],
        out_specs=whole,
        out_shape=jax.ShapeDtypeStruct((total, d), F32),
        scratch_shapes=[pltpu.VMEM((tm, d), BF16)],
        compiler_params=_params(("parallel", "arbitrary")),
        name="ffn_half",
    )(*xs, g, wg, wu, wd)


def _ffn_half_split_out(x, rows_first, g, wg, wu, wd, tm):
    total, d = x.shape
    fp = wg.shape[1]
    split = rows_first // tm
    return pl.pallas_call(
        functools.partial(_ffn_kernel, n_in=1, n_out=2, split=split),
        grid=(total // tm, fp // TF),
        in_specs=[
            pl.BlockSpec((tm, d), lambda i, j: (i, 0)),
            pl.BlockSpec((1, d), lambda i, j: (0, 0)),
            pl.BlockSpec((d, TF), lambda i, j: (0, j)),
            pl.BlockSpec((d, TF), lambda i, j: (0, j)),
            pl.BlockSpec((TF, d), lambda i, j: (j, 0)),
        ],
        out_specs=[
            pl.BlockSpec((tm, d), lambda i, j: (jnp.minimum(i, split - 1), 0)),
            pl.BlockSpec((tm, d), lambda i, j: (jnp.maximum(i - split, 0), 0)),
        ],
        out_shape=[jax.ShapeDtypeStruct((rows_first, d), F32),
                   jax.ShapeDtypeStruct((total - rows_first, d), F32)],
        scratch_shapes=[pltpu.VMEM((tm, d), BF16)],
        compiler_params=_params(("arbitrary", "arbitrary")),
        name="ffn_half_out",
    )(x, g, wg, wu, wd)


def _proj_kernel(x_ref, g_ref, w_ref, o_ref):
    n = _rms(x_ref[...], g_ref[...]).astype(BF16)
    for c in range(0, w_ref.shape[1], TN):
        o_ref[:, c:c + TN] = jnp.dot(n, w_ref[:, c:c + TN], preferred_element_type=F32).astype(o_ref.dtype)


def _norm_proj(x, g, w, out_dtype, name):
    t, d = x.shape
    n = w.shape[1]
    return pl.pallas_call(
        _proj_kernel,
        grid=(t // TM,),
        in_specs=[
            pl.BlockSpec((TM, d), lambda i: (i, 0)),
            _resident((1, d)),
            _resident((d, n)),
        ],
        out_specs=pl.BlockSpec((TM, n), lambda i: (i, 0)),
        out_shape=jax.ShapeDtypeStruct((t, n), out_dtype),
        compiler_params=_params(("parallel",)),
        name=name,
    )(x, g, w)


def _qkv_kernel(x_ref, g_ref, w_ref, hg_ref, cos_ref, sin_ref, o_ref, *, n_rope):
    n = _rms(x_ref[...], g_ref[...]).astype(BF16)
    u = jnp.dot(n, w_ref[...], preferred_element_type=F32)
    c = cos_ref[...]
    s = sin_ref[...]
    for h in range(u.shape[1] // HEAD_DIM):
        sl = slice(h * HEAD_DIM, (h + 1) * HEAD_DIM)
        uh = u[:, sl]
        if h < n_rope:
            ms = jnp.mean(uh * uh, axis=-1, keepdims=True)
            uh = uh * lax.rsqrt(ms + NORM_EPS) * hg_ref[:, sl]
            uh = uh * c + pltpu.roll(uh, HEAD_DIM // 2, 1) * s
        o_ref[:, sl] = uh.astype(o_ref.dtype)


def _qkv_proj(x, g, w, head_gain, cos, sin, seq_len):
    t, d = x.shape
    n = w.shape[1]
    tiles_per_seq = seq_len // TM
    n_rope = N_Q_HEADS + N_KV_HEADS
    return pl.pallas_call(
        functools.partial(_qkv_kernel, n_rope=n_rope),
        grid=(t // TM,),
        in_specs=[
            pl.BlockSpec((TM, d), lambda i: (i, 0)),
            _resident((1, d)),
            _resident((d, n)),
            _resident((1, n)),
            pl.BlockSpec((TM, HEAD_DIM), lambda i: (i % tiles_per_seq, 0)),
            pl.BlockSpec((TM, HEAD_DIM), lambda i: (i % tiles_per_seq, 0)),
        ],
        out_specs=pl.BlockSpec((TM, n), lambda i: (i, 0)),
        out_shape=jax.ShapeDtypeStruct((t, n), BF16),
        compiler_params=_params(("parallel",)),
        name="qkv_proj",
    )(x, g, w, head_gain, cos, sin)


def _outproj_kernel(*refs, n_y):
    r_ref = refs[0]
    y_refs = refs[1:1 + n_y]
    w_ref = refs[1 + n_y]
    o_ref = refs[2 + n_y]
    ys = [y_ref[...] for y_ref in y_refs]
    for c in range(0, o_ref.shape[1], TN):
        acc = r_ref[:, c:c + TN]
        k0 = 0
        for y in ys:
            k = y.shape[1]
            acc = acc + jnp.dot(y, w_ref[k0:k0 + k, c:c + TN], preferred_element_type=F32)
            k0 += k
        o_ref[:, c:c + TN] = acc


def _out_proj(res, ys, w, name):
    t, d = res.shape
    return pl.pallas_call(
        functools.partial(_outproj_kernel, n_y=len(ys)),
        grid=(t // TM,),
        in_specs=[pl.BlockSpec((TM, d), lambda i: (i, 0))]
        + [pl.BlockSpec((TM, y.shape[1]), lambda i: (i, 0)) for y in ys]
        + [_resident(w.shape)],
        out_specs=pl.BlockSpec((TM, d), lambda i: (i, 0)),
        out_shape=jax.ShapeDtypeStruct((t, d), F32),
        compiler_params=_params(("parallel",)),
        name=name,
    )(res, *ys, w)


def _attn_kernel(q_ref, k_ref, v_ref, o_ref):
    k = k_ref[...]
    v = v_ref[...]
    for g in range(q_ref.shape[1] // HEAD_DIM):
        sl = slice(g * HEAD_DIM, (g + 1) * HEAD_DIM)
        st = lax.dot_general(k, q_ref[:, sl], (((1,), (1,)), ((), ())), preferred_element_type=F32)
        m = jnp.max(st, axis=0, keepdims=True)
        p = jnp.exp(st - m)
        l = jnp.sum(p, axis=0, keepdims=True)
        ot = lax.dot_general(v, p.astype(BF16), (((0,), (0,)), ((), ())), preferred_element_type=F32)
        o_ref[:, sl] = (ot / l).T.astype(o_ref.dtype)


def _attention(qkv, batch, seq_len, tq):
    t = qkv.shape[0]
    gw = GQA_GROUP * HEAD_DIM
    nq = seq_len // tq
    k_off = N_Q_HEADS
    v_off = N_Q_HEADS + N_KV_HEADS
    return pl.pallas_call(
        _attn_kernel,
        grid=(batch, N_KV_HEADS, nq),
        in_specs=[
            pl.BlockSpec((tq, gw), lambda b, h, i: (b * nq + i, h)),
            pl.BlockSpec((seq_len, HEAD_DIM), lambda b, h, i: (b, k_off + h)),
            pl.BlockSpec((seq_len, HEAD_DIM), lambda b, h, i: (b, v_off + h)),
        ],
        out_specs=pl.BlockSpec((tq, gw), lambda b, h, i: (b * nq + i, h)),
        out_shape=jax.ShapeDtypeStruct((t, N_Q_HEADS * HEAD_DIM), BF16),
        compiler_params=_params(("parallel", "parallel", "arbitrary")),
        name="attention",
    )(qkv, qkv, qkv)


def _fill_padded(src_ref, pad_ref):
    seq = src_ref.shape[0]
    zeros = jnp.zeros((SUBLANES, pad_ref.shape[1]), F32)
    pad_ref[0:SUBLANES, :] = zeros
    pad_ref[seq + SUBLANES:seq + 2 * SUBLANES, :] = zeros
    pad_ref[SUBLANES:seq + SUBLANES, :] = src_ref[...]


def _short_conv_rows(pad_ref, w, b, r0):
    n = CONV_ROWS + 2 * SUBLANES
    win = pad_ref[pl.ds(r0, n), :]
    taps = (pltpu.roll(win, 1, 0), win, pltpu.roll(win, n - 1, 0))
    y = b
    for j in range(HY_SHORT_W):
        y = y + taps[j][SUBLANES:SUBLANES + CONV_ROWS, :] * w[j:j + 1, :]
    return y


def _filter_kernel(z_ref, w1_ref, b1_ref, w2_ref, b2_ref, w3_ref, fr_ref, dl_ref, ks_ref, kd_ref, *, seq_len):
    hi = lax.Precision.HIGHEST
    fr = fr_ref[...]
    h = jnp.sin(fr * (jnp.dot(z_ref[...], w1_ref[...], precision=hi, preferred_element_type=F32) + b1_ref[...]))
    for i in range(HY_N_INNER):
        h = jnp.sin(fr * (jnp.dot(h, w2_ref[i], precision=hi, preferred_element_type=F32) + b2_ref[i]))
    k = jnp.dot(h, w3_ref[...], precision=hi, preferred_element_type=F32)
    tl, half = ks_ref.shape
    pos = lax.broadcasted_iota(jnp.int32, (tl, half), 0) + pl.program_id(0) * tl
    tnorm = pos.astype(F32) / (seq_len - 1)
    decay = jnp.exp(-tnorm * dl_ref[...])
    kf = k[:, :half] * decay
    kb = jnp.where(pos == 0, 0.0, k[:, half:] * decay)
    ks_ref[...] = (kf + kb).astype(ks_ref.dtype)
    kd_ref[...] = (kf - kb).astype(kd_ref.dtype)


def _hyena_filter_taps(zfeat, w1, b1, w2, b2, w3, freq, deltas2, seq_len, tl):
    half = deltas2.shape[1]
    return pl.pallas_call(
        functools.partial(_filter_kernel, seq_len=seq_len),
        grid=(seq_len // tl,),
        in_specs=[
            pl.BlockSpec((tl, zfeat.shape[1]), lambda i: (i, 0)),
            _resident(w1.shape),
            _resident(b1.shape),
            _resident(w2.shape),
            _resident(b2.shape),
            _resident(w3.shape),
            _resident(freq.shape),
            _resident(deltas2.shape),
        ],
        out_specs=[pl.BlockSpec((tl, half), lambda i: (i, 0))] * 2,
        out_shape=[jax.ShapeDtypeStruct((seq_len, half), BF16)] * 2,
        compiler_params=_params(("parallel",)),
        name="hyena_filter_taps",
    )(zfeat, w1, b1, w2, b2, w3, freq, deltas2)


def _spectrum_kernel(p_ref, ks_ref, kd_ref, ka_ref, kc_ref, ny_ref, *, period):
    fb = ka_ref.shape[0]
    ks = ks_ref[...]
    re = jnp.dot(p_ref[0:fb, :], ks, preferred_element_type=F32)
    im = jnp.dot(p_ref[fb:2 * fb, :], kd_ref[...], preferred_element_type=F32)
    ny = jnp.dot(p_ref[fb:fb + SUBLANES, :], ks, preferred_element_type=F32)
    is_dc = (lax.broadcasted_iota(jnp.int32, re.shape, 0) + pl.program_id(1) * fb) == 0
    ka_ref[...] = re * jnp.where(is_dc, 1.0 / period, 2.0 / period)
    kc_ref[...] = jnp.where(is_dc, 0.0, im * (2.0 / period))
    ny_ref[...] = ny * (1.0 / period)


def _hyena_spectrum(p, ksum, kdiff, tcs):
    seq_len, cols = ksum.shape
    nf = p.shape[0] // (2 * FB)
    return pl.pallas_call(
        functools.partial(_spectrum_kernel, period=2 * seq_len),
        grid=(cols // tcs, nf),
        in_specs=[
            pl.BlockSpec((2 * FB, seq_len), lambda c, f: (f, 0)),
            pl.BlockSpec((seq_len, tcs), lambda c, f: (0, c)),
            pl.BlockSpec((seq_len, tcs), lambda c, f: (0, c)),
        ],
        out_specs=[
            pl.BlockSpec((FB, tcs), lambda c, f: (f, c)),
            pl.BlockSpec((FB, tcs), lambda c, f: (f, c)),
            pl.BlockSpec((SUBLANES, tcs), lambda c, f: (f, c)),
        ],
        out_shape=[
            jax.ShapeDtypeStruct((nf * FB, cols), F32),
            jax.ShapeDtypeStruct((nf * FB, cols), F32),
            jax.ShapeDtypeStruct((nf * SUBLANES, cols), F32),
        ],
        compiler_params=_params(("parallel", "parallel")),
        name="hyena_filter_spectrum",
    )(p, ksum, kdiff)


def _longconv_kernel(z_ref, gate_ref, swz_ref, sbz_ref, swg_ref, sbg_ref, p_ref, ka_ref, kc_ref, ny_ref,
                     bias_ref, o_ref, pad_ref, zc_ref, zb_ref, acc_ref, *, conv_z):
    f = pl.program_id(2)
    fb = ka_ref.shape[0]
    seq = z_ref.shape[0]

    @pl.when(f == 0)
    def _():
        if conv_z:
            _fill_padded(z_ref, pad_ref)
            w = swz_ref[...]
            b = sbz_ref[...]

            def body(i, carry):
                r0 = pl.multiple_of(i * CONV_ROWS, CONV_ROWS)
                y = _short_conv_rows(pad_ref, w, b, r0)
                zc_ref[pl.ds(r0, CONV_ROWS), :] = y
                zb_ref[pl.ds(r0, CONV_ROWS), :] = y.astype(BF16)
                return carry

            lax.fori_loop(0, seq // CONV_ROWS, body, 0)
        else:
            zb_ref[...] = z_ref[...].astype(BF16)
        acc_ref[...] = jnp.zeros_like(acc_ref)

    spec = jnp.dot(p_ref[...], zb_ref[...], preferred_element_type=F32)
    xre = spec[:fb]
    xim = spec[fb:]
    ka = ka_ref[...]
    kc = kc_ref[...]
    is_dc = (lax.broadcasted_iota(jnp.int32, ka.shape, 0) + f * fb) == 0
    kd = jnp.where(is_dc, ny_ref[0:1, :], ka)
    yre = (xre * ka - xim * kc).astype(BF16)
    yim = (xre * kc + xim * kd).astype(BF16)
    tn = (((0,), (0,)), ((), ()))
    acc_ref[...] += (lax.dot_general(p_ref[0:fb, :], yre, tn, preferred_element_type=F32)
                     + lax.dot_general(p_ref[fb:2 * fb, :], yim, tn, preferred_element_type=F32))

    @pl.when(f == pl.num_programs(2) - 1)
    def _():
        _fill_padded(gate_ref, pad_ref)
        w = swg_ref[...]
        b = sbg_ref[...]
        bias = bias_ref[...]

        def body(i, carry):
            r0 = pl.multiple_of(i * CONV_ROWS, CONV_ROWS)
            g = _short_conv_rows(pad_ref, w, b, r0)
            zsrc = zc_ref if conv_z else z_ref
            zv = zsrc[pl.ds(r0, CONV_ROWS), :]
            o_ref[pl.ds(r0, CONV_ROWS), :] = (g * (acc_ref[pl.ds(r0, CONV_ROWS), :] + zv * bias)).astype(o_ref.dtype)
            return carry

        lax.fori_loop(0, seq // CONV_ROWS, body, 0)


def _long_conv(z, z_col, gate, gate_col, short_w, short_b, p, ka, kc, ny, bias, order, conv_z,
               batch, seq_len, tc, out_dtype):
    c = bias.shape[1] // HY_ORDER
    t = z.shape[0]
    nc = c // tc
    nf = p.shape[0] // (2 * FB)
    zc_col = z_col if conv_z else 0
    return pl.pallas_call(
        functools.partial(_longconv_kernel, conv_z=conv_z),
        grid=(batch, nc, nf),
        in_specs=[
            pl.BlockSpec((seq_len, tc), lambda b, j, f: (b, z_col * nc + j)),
            pl.BlockSpec((seq_len, tc), lambda b, j, f: (b, gate_col * nc + j)),
            pl.BlockSpec((HY_SHORT_W, tc), lambda b, j, f: (0, zc_col * nc + j)),
            pl.BlockSpec((1, tc), lambda b, j, f: (0, zc_col * nc + j)),
            pl.BlockSpec((HY_SHORT_W, tc), lambda b, j, f: (0, gate_col * nc + j)),
            pl.BlockSpec((1, tc), lambda b, j, f: (0, gate_col * nc + j)),
            pl.BlockSpec((2 * FB, seq_len), lambda b, j, f: (f, 0)),
            pl.BlockSpec((FB, tc), lambda b, j, f: (f, order * nc + j)),
            pl.BlockSpec((FB, tc), lambda b, j, f: (f, order * nc + j)),
            pl.BlockSpec((SUBLANES, tc), lambda b, j, f: (0, order * nc + j)),
            pl.BlockSpec((1, tc), lambda b, j, f: (0, order * nc + j)),
        ],
        out_specs=pl.BlockSpec((seq_len, tc), lambda b, j, f: (b, j)),
        out_shape=jax.ShapeDtypeStruct((t, c), out_dtype),
        scratch_shapes=[
            pltpu.VMEM((seq_len + 2 * SUBLANES, tc), F32),
            pltpu.VMEM((seq_len, tc) if conv_z else (SUBLANES, LANES), F32),
            pltpu.VMEM((seq_len, tc), BF16),
            pltpu.VMEM((seq_len, tc), F32),
        ],
        compiler_params=_params(("parallel", "parallel", "arbitrary")),
        name="hyena_long_conv",
    )(z, gate, short_w, short_b, short_w, short_b, p, ka, kc, ny, bias)


def _dft_matrix(seq_len):
    period = 2 * seq_len
    split = 64
    f = jnp.arange(seq_len, dtype=jnp.int32)[:, None]

    def trig(tvals):
        ang = ((f * tvals[None, :]) % period).astype(F32) * (2.0 * math.pi / period)
        return jnp.cos(ang), jnp.sin(ang)

    ca, sa = trig(jnp.arange(seq_len // split, dtype=jnp.int32) * split)
    cb, sb = trig(jnp.arange(split, dtype=jnp.int32))
    re = (ca[:, :, None] * cb[:, None, :] - sa[:, :, None] * sb[:, None, :]).reshape(seq_len, seq_len)
    im = -(sa[:, :, None] * cb[:, None, :] + ca[:, :, None] * sb[:, None, :]).reshape(seq_len, seq_len)
    nyq = jnp.where(jnp.arange(seq_len) % 2 == 0, 1.0, -1.0).astype(F32)
    im = im.at[0].set(nyq)
    nf = seq_len // FB
    p = jnp.stack([re.reshape(nf, FB, seq_len), im.reshape(nf, FB, seq_len)], axis=1)
    return p.reshape(2 * seq_len, seq_len).astype(BF16)


def _position_features(seq_len, width):
    pos = jnp.arange(seq_len, dtype=F32)
    t = pos / (seq_len - 1)
    w = 2.0 * math.pi * pos / seq_len
    f = jnp.linspace(1e-4, HY_BANDS - 1, HY_BANDS, dtype=F32)
    fw = w[:, None] * f[None, :]
    z = jnp.concatenate([t[:, None], jnp.cos(fw), -jnp.sin(fw)], axis=-1)
    return jnp.pad(z, ((0, 0), (0, width - z.shape[1])))


def _softplus(x):
    return jnp.maximum(x, 0.0) + jnp.log1p(jnp.exp(-jnp.abs(x)))


def _gelu_tanh(x):
    return 0.5 * x * (1.0 + jnp.tanh(math.sqrt(2.0 / math.pi) * (x + 0.044715 * (x * x * x))))


def _rglru_kernel(g_ref, x_ref, cw_ref, cb_ref, w_ref, bias_ref, lam_ref, o_ref,
                  xp_ref, xq_ref, a_ref, b_ref, hl_ref, al_ref, hs_ref):
    seq, width = x_ref.shape
    nck = SUBLANES
    chunk = seq // nck
    pitch = chunk + SCAN_PITCH_PAD
    slabs = width // LANES
    halo = RNN_CONV_W - 1

    for c in range(nck):
        for sl in range(slabs):
            xp_ref[sl, c * pitch:c * pitch + chunk, :] = x_ref[c * chunk:(c + 1) * chunk, sl * LANES:(sl + 1) * LANES]

    def to_chunk_major(i, carry):
        s0 = i * SCAN_GROUP
        row0 = pl.multiple_of((s0 + halo) * nck, nck)
        for k in range(SCAN_GROUP):
            for sl in range(slabs):
                xq_ref[pl.ds(row0 + k * nck, nck), sl * LANES:(sl + 1) * LANES] = (
                    xp_ref[sl, pl.ds(s0 + k, nck, stride=pitch), :])
        return carry

    lax.fori_loop(0, chunk // SCAN_GROUP, to_chunk_major, 0)

    sub = lax.broadcasted_iota(jnp.int32, (nck, width), 0)
    for k in range(halo):
        tail = xq_ref[(chunk + k) * nck:(chunk + k + 1) * nck, :]
        xq_ref[k * nck:(k + 1) * nck, :] = jnp.where(sub == 0, 0.0, pltpu.roll(tail, 1, 0))
        head = xq_ref[(halo + k) * nck:(halo + k + 1) * nck, :]
        xq_ref[(halo + chunk + k) * nck:(halo + chunk + k + 1) * nck, :] = jnp.where(
            sub == nck - 1, 0.0, pltpu.roll(head, nck - 1, 0))

    for d in range(2):
        ch = (0.5 * RNN_C) * _softplus(-lam_ref[d:d + 1, :])
        reset_row = 0 if d == 0 else seq - 1

        def gates(bi, carry):
            base = pl.multiple_of(bi * GATE_ROWS, GATE_ROWS)
            xc = cb_ref[d:d + 1, :]
            for j in range(RNN_CONV_W):
                step = j if d == 0 else 2 * halo - j
                xc = xc + xq_ref[pl.ds(base + step * nck, GATE_ROWS), :] * cw_ref[d, j:j + 1, :]
            gt = jnp.dot(xc.astype(BF16), w_ref[d, 0], preferred_element_type=F32) + bias_ref[d, 0]
            tr = jnp.tanh(gt[:, :width])
            ti = jnp.tanh(gt[:, width:])
            nla = tr * ch + ch
            a = jnp.exp2(nla * (-LOG2E))
            m2 = jnp.tanh(nla) * (a * a + 1.0)
            mult = jnp.where(m2 > 0.0, m2 * lax.rsqrt(m2), 0.0)
            ixc = (0.5 * ti + 0.5) * xc
            bt = mult * ixc
            a_ref[d, pl.ds(base, GATE_ROWS), :] = a
            b_ref[d, pl.ds(base, GATE_ROWS), :] = bt

            @pl.when(bi == reset_row // GATE_ROWS)
            def _():
                lo = (reset_row % GATE_ROWS) // nck * nck
                fixed = jnp.where(sub == reset_row % nck, ixc[lo:lo + nck, :], bt[lo:lo + nck, :])
                b_ref[d, pl.ds(base + lo, nck), :] = fixed

            return carry

        lax.fori_loop(0, seq // GATE_ROWS, gates, 0)

    def scan(i, carry):
        state = list(carry)
        group_rows = SCAN_GROUP * nck
        for k in range(SCAN_GROUP):
            for d in range(2):
                h, acc = state[2 * d], state[2 * d + 1]
                if d == 0:
                    row = pl.multiple_of(i * group_rows, nck) + k * nck
                else:
                    row = pl.multiple_of((chunk - SCAN_GROUP) * nck - i * group_rows, nck) + (SCAN_GROUP - 1 - k) * nck
                av = a_ref[d, pl.ds(row, nck), :]
                h = av * h + b_ref[d, pl.ds(row, nck), :]
                acc = acc * av
                hl_ref[d, pl.ds(row, nck), :] = h
                al_ref[d, pl.ds(row, nck), :] = acc
                state[2 * d], state[2 * d + 1] = h, acc
        return tuple(state)

    init = (jnp.zeros((nck, width), F32), jnp.ones((nck, width), F32)) * 2
    ends = lax.fori_loop(0, chunk // SCAN_GROUP, scan, init)

    carries = []
    for d in range(2):
        h_end, a_end = ends[2 * d], ends[2 * d + 1]
        rows = [None] * nck
        cur = jnp.zeros((1, width), F32)
        for c in (range(nck) if d == 0 else range(nck - 1, -1, -1)):
            rows[c] = cur
            cur = a_end[c:c + 1, :] * cur + h_end[c:c + 1, :]
        carries.append(jnp.concatenate(rows, axis=0))

    def fold(i, carry):
        s0 = i * SCAN_GROUP
        row0 = pl.multiple_of(s0 * nck, nck)
        for k in range(SCAN_GROUP):
            rows = pl.ds(row0 + k * nck, nck)
            hv = (hl_ref[0, rows, :] + al_ref[0, rows, :] * carries[0]
                  + hl_ref[1, rows, :] + al_ref[1, rows, :] * carries[1])
            for sl in range(slabs):
                hs_ref[sl, pl.ds(s0 + k, nck, stride=pitch), :] = hv[:, sl * LANES:(sl + 1) * LANES]
        return carry

    lax.fori_loop(0, chunk // SCAN_GROUP, fold, 0)

    def finish(c, carry):
        r0 = pl.multiple_of(c * chunk, chunk)
        p0 = pl.multiple_of(c * pitch, SUBLANES)
        gate = _gelu_tanh(g_ref[pl.ds(r0, chunk), :])
        for sl in range(slabs):
            o_ref[pl.ds(r0, chunk), sl * LANES:(sl + 1) * LANES] = (
                hs_ref[sl, pl.ds(p0, chunk), :] * gate[:, sl * LANES:(sl + 1) * LANES]).astype(o_ref.dtype)
        return carry

    lax.fori_loop(0, nck, finish, 0)


def _rglru(u, conv_w, conv_b, w_gates, b_gates, lam, batch, seq_len):
    t, two_d = u.shape
    d = two_d // 2
    width = d // RNN_HEADS
    slabs = width // LANES
    chunk = seq_len // SUBLANES
    pitched = SUBLANES * (chunk + SCAN_PITCH_PAD)
    halo_rows = 2 * (RNN_CONV_W - 1) * SUBLANES
    return pl.pallas_call(
        _rglru_kernel,
        grid=(batch, RNN_HEADS),
        in_specs=[
            pl.BlockSpec((seq_len, width), lambda b, h: (b, h)),
            pl.BlockSpec((seq_len, width), lambda b, h: (b, RNN_HEADS + h)),
            pl.BlockSpec((2, RNN_CONV_W, width), lambda b, h: (0, 0, h)),
            pl.BlockSpec((2, width), lambda b, h: (0, h)),
            pl.BlockSpec((2, 1, width, 2 * width), lambda b, h: (0, h, 0, 0)),
            pl.BlockSpec((2, 1, 1, 2 * width), lambda b, h: (0, h, 0, 0)),
            pl.BlockSpec((2, width), lambda b, h: (0, h)),
        ],
        out_specs=pl.BlockSpec((seq_len, width), lambda b, h: (b, h)),
        out_shape=jax.ShapeDtypeStruct((t, d), BF16),
        scratch_shapes=[
            pltpu.VMEM((slabs, pitched, LANES), F32),
            pltpu.VMEM((seq_len + halo_rows, width), F32),
            pltpu.VMEM((2, seq_len, width), F32),
            pltpu.VMEM((2, seq_len, width), F32),
            pltpu.VMEM((2, seq_len, width), F32),
            pltpu.VMEM((2, seq_len, width), F32),
            pltpu.VMEM((slabs, pitched, LANES), F32),
        ],
        compiler_params=_params(("parallel", "parallel")),
        name="rglru",
    )(u, u, conv_w, conv_b, w_gates, b_gates, lam)


def _deinterleave_heads(w, n_heads):
    lead = w.shape[:-1]
    w = w.reshape(lead + (n_heads, HEAD_DIM // 2, 2))
    w = jnp.swapaxes(w, -1, -2)
    return w.reshape(lead + (n_heads * HEAD_DIM,))


def _rope_tables(seq_len):
    rows = seq_len // GRID_W
    row = jnp.repeat(jnp.arange(rows, dtype=F32), GRID_W)
    col = jnp.tile(jnp.arange(GRID_W, dtype=F32), rows)
    axis_dim = HEAD_DIM // 2
    omega = ROPE_THETA ** (-jnp.arange(0, axis_dim, 2, dtype=F32) / axis_dim)
    ang = jnp.concatenate([row[:, None] * omega[None], col[:, None] * omega[None]], axis=-1)
    c, s = jnp.cos(ang), jnp.sin(ang)
    return jnp.concatenate([c, c], axis=-1), jnp.concatenate([-s, s], axis=-1)


def _even_mixer(x, batch, seq_len, norm_g, w_in, short_w, short_b, f_w1, f_b1, f_w2, f_b2, f_w3, f_freq,
                hy_bias, q_g, k_g, w_out):
    d = x.shape[1]
    hy_d = hy_bias.shape[1]
    s0 = (HY_ORDER + 1) * hy_d
    nq = N_Q_HEADS * HEAD_DIM
    nk = N_KV_HEADS * HEAD_DIM
    g = norm_g.reshape(1, d)

    u_hy = _norm_proj(x, g, w_in[:, :s0].astype(BF16), F32, "hyena_in_proj")

    hid = LANES
    hpad = hid - f_w1.shape[1]
    zfeat = _position_features(seq_len, LANES)
    w1 = jnp.pad(f_w1, ((0, LANES - f_w1.shape[0]), (0, hpad)))
    b1 = jnp.pad(f_b1, (0, hpad)).reshape(1, hid)
    w2 = jnp.pad(f_w2, ((0, 0), (0, hpad), (0, hpad)))
    b2 = jnp.pad(f_b2, ((0, 0), (0, hpad))).reshape(HY_N_INNER, 1, hid)
    w3 = jnp.pad(f_w3, ((0, hpad), (0, 0)))
    freq = jnp.pad(f_freq, (0, hpad)).reshape(1, hid)
    max_decay = math.log(HY_DECAY_TARGET) / HY_FAST_DECAY
    min_decay = math.log(HY_DECAY_TARGET) / HY_SLOW_DECAY
    deltas = jnp.abs(jnp.linspace(min_decay, max_decay, hy_d, dtype=F32))
    deltas2 = jnp.tile(deltas, HY_ORDER).reshape(1, HY_ORDER * hy_d)
    ksum, kdiff = _hyena_filter_taps(zfeat, w1, b1, w2, b2, w3, freq, deltas2, seq_len, 256)
    p = _dft_matrix(seq_len)
    ka, kc, ny = _hyena_spectrum(p, ksum, kdiff, 512)
    bias = hy_bias.reshape(1, HY_ORDER * hy_d)
    sb = short_b.reshape(1, s0)
    z1 = _long_conv(u_hy, 0, u_hy, 1, short_w, sb, p, ka, kc, ny, bias, 0, True, batch, seq_len, 512, F32)
    y_hy = _long_conv(z1, 0, u_hy, 2, short_w, sb, p, ka, kc, ny, bias, 1, False, batch, seq_len, 512, BF16)

    w_qkv = jnp.concatenate([
        _deinterleave_heads(w_in[:, s0:s0 + nq], N_Q_HEADS),
        _deinterleave_heads(w_in[:, s0 + nq:s0 + nq + nk], N_KV_HEADS),
        w_in[:, s0 + nq + nk:]], axis=1).astype(BF16)
    qg = _deinterleave_heads(q_g, 1) * (HEAD_DIM ** -0.5)
    kg = _deinterleave_heads(k_g, 1)
    head_gain = jnp.concatenate([jnp.tile(qg, N_Q_HEADS), jnp.tile(kg, N_KV_HEADS),
                                 jnp.ones((nk,), F32)]).reshape(1, nq + 2 * nk)
    cos, sin = _rope_tables(seq_len)
    qkv = _qkv_proj(x, g, w_qkv, head_gain, cos, sin, seq_len)
    y_at = _attention(qkv, batch, seq_len, 512)

    return _out_proj(x, [y_hy, y_at], w_out.astype(BF16), "even_out_proj")


def _odd_mixer(x, batch, seq_len, norm_g, w_in, conv_w, conv_b, wa, ba, wx, bx, lam, w_out):
    d = x.shape[1]
    u = _norm_proj(x, norm_g.reshape(1, d), w_in.astype(BF16), F32, "rglru_in_proj")
    w_gates = (0.5 * jnp.concatenate([wa, wx], axis=-1)).astype(BF16)
    b_gates = 0.5 * jnp.concatenate([ba, bx], axis=-1)[:, :, None, :]
    y = _rglru(u, conv_w, conv_b, w_gates, b_gates, lam, batch, seq_len)
    return _out_proj(x, [y], w_out.astype(BF16), "rglru_out_proj")


def kernel(x_prompt, x_sample, ffn_norm, ffn_w_gate, ffn_w_up, ffn_w_down, even_norm, even_w_in, hy_short_w, hy_short_b, hy_filt_w1, hy_filt_b1, hy_filt_w2, hy_filt_b2, hy_filt_w3, hy_filt_freq, hy_bias, q_norm, k_norm, even_w_out, odd_norm, odd_w_in, rg_conv_w, rg_conv_b, rg_wa, rg_ba, rg_wx, rg_bx, rg_lambda, odd_w_out):
    bp, seq_len, d = x_prompt.shape
    bs = x_sample.shape[0]
    assert x_sample.shape[1:] == (seq_len, d)
    batch = bp + bs
    rows_p = bp * seq_len

    depth = ffn_norm.shape[0]
    ff = ffn_w_gate.shape[-1]
    fp = -(-ff // TF) * TF
    wg = _cast_pad(ffn_w_gate.reshape(depth * 2, d, ff), d, fp, 1024, TF).reshape(depth, 2, d, fp)
    wu = _cast_pad(ffn_w_up.reshape(depth * 2, d, ff), d, fp, 1024, TF).reshape(depth, 2, d, fp)
    wd = _cast_pad(ffn_w_down.reshape(depth * 2, ff, d), fp, d, TF, d).reshape(depth, 2, fp, d)

    x = None
    for layer in range(depth):
        g0 = ffn_norm[layer, 0].reshape(1, d)
        if layer == 0:
            x = _ffn_half([x_prompt.reshape(rows_p, d), x_sample.reshape(bs * seq_len, d)], g0,
                          wg[layer, 0], wu[layer, 0], wd[layer, 0], TM)
        else:
            x = _ffn_half([x], g0, wg[layer, 0], wu[layer, 0], wd[layer, 0], TM_FFN)
        j = layer // 2
        if layer % 2 == 0:
            x = _even_mixer(x, batch, seq_len, even_norm[j], even_w_in[j], hy_short_w[j], hy_short_b[j],
                            hy_filt_w1[j], hy_filt_b1[j], hy_filt_w2[j], hy_filt_b2[j], hy_filt_w3[j],
                            hy_filt_freq[j], hy_bias[j], q_norm[j], k_norm[j], even_w_out[j])
        else:
            x = _odd_mixer(x, batch, seq_len, odd_norm[j], odd_w_in[j], rg_conv_w[j], rg_conv_b[j],
                           rg_wa[j], rg_ba[j], rg_wx[j], rg_bx[j], rg_lambda[j], odd_w_out[j])
        g1 = ffn_norm[layer, 1].reshape(1, d)
        if layer == depth - 1:
            y_prompt, y_sample = _ffn_half_split_out(x, rows_p, g1, wg[layer, 1], wu[layer, 1], wd[layer, 1], TM)
        else:
            x = _ffn_half([x], g1, wg[layer, 1], wu[layer, 1], wd[layer, 1], TM_FFN)

    return (y_prompt.reshape(bp, seq_len, d), y_sample.reshape(bs, seq_len, d))
```

```python
import functools
import math

import jax
import jax.numpy as jnp
from jax import lax
from jax.experimental import pallas as pl
from jax.experimental.pallas import tpu as pltpu

F32 = jnp.float32
BF16 = jnp.bfloat16

NORM_EPS = 1e-6
GRID_W = 64
HEAD_DIM = 128
N_Q_HEADS = 8
N_KV_HEADS = 2
GQA_GROUP = N_Q_HEADS // N_KV_HEADS
ROPE_THETA = 10000.0
HY_ORDER = 2
HY_SHORT_W = 3
HY_EMB_DIM = 33
HY_BANDS = (HY_EMB_DIM - 1) // 2
HY_N_INNER = 2
HY_FAST_DECAY = 0.3
HY_SLOW_DECAY = 1.5
HY_DECAY_TARGET = 1e-2
RNN_HEADS = 8
RNN_CONV_W = 4
RNN_C = 8.0
LOG2E = 1.4426950408889634

LANES = 128
SUBLANES = 8
VMEM_LIMIT = 56 * 1024 * 1024

TM = 512
TM_FFN = 1024
TF = 512
TN = 1024
ATT_KC = 256
FB = 256
CONV_ROWS = 64
GATE_ROWS = 256
SCAN_GROUP = 8
SCAN_PITCH_PAD = 8


def _params(sem, vmem=VMEM_LIMIT):
    return pltpu.CompilerParams(dimension_semantics=sem, vmem_limit_bytes=vmem)


def _resident(shape):
    return pl.BlockSpec(shape, lambda *_: (0,) * len(shape), pipeline_mode=pl.Buffered(1))


def _rms(x, g):
    ms = jnp.mean(x * x, axis=-1, keepdims=True)
    return x * lax.rsqrt(ms + NORM_EPS) * g


def _cast_pad_kernel(w_ref, o_ref, *, rows, cols):
    br, bc = o_ref.shape
    r = lax.broadcasted_iota(jnp.int32, (br, bc), 0) + pl.program_id(1) * br
    c = lax.broadcasted_iota(jnp.int32, (br, bc), 1) + pl.program_id(2) * bc
    w = jnp.where(r < rows, jnp.where(c < cols, w_ref[...], 0.0), 0.0)
    o_ref[...] = w.astype(o_ref.dtype)


def _cast_pad(w, rows_out, cols_out, br, bc):
    n, rows, cols = w.shape
    return pl.pallas_call(
        functools.partial(_cast_pad_kernel, rows=rows, cols=cols),
        grid=(n, rows_out // br, cols_out // bc),
        in_specs=[pl.BlockSpec((None, br, bc), lambda k, i, j: (k, i, j))],
        out_specs=pl.BlockSpec((None, br, bc), lambda k, i, j: (k, i, j)),
        out_shape=jax.ShapeDtypeStruct((n, rows_out, cols_out), BF16),
        compiler_params=_params(("parallel", "parallel", "parallel")),
        name="cast_pad",
    )(w)


def _ffn_kernel(x_ref, g_ref, wg_ref, wu_ref, wd_ref, *rest):
    o_ref, n_ref = rest[-2:]

    @pl.when(pl.program_id(1) == 0)
    def _():
        x = x_ref[...]
        n_ref[...] = _rms(x, g_ref[...]).astype(BF16)
        o_ref[...] = x

    n = n_ref[...]
    hg = jnp.dot(n, wg_ref[...], preferred_element_type=F32)
    hu = jnp.dot(n, wu_ref[...], preferred_element_type=F32)
    h = ((hg * jax.nn.sigmoid(hg)) * (hu * 0.5)).astype(BF16)
    o_ref[...] += jnp.dot(h, wd_ref[...], preferred_element_type=F32)


def _ffn_half(x, g, w_idx, wg, wu, wd, *, rows=None, x_row0=0, out_rows=None, out_row0=0, dest=None):
    d = x.shape[1]
    fp = wg.shape[2]
    rows = x.shape[0] if rows is None else rows
    out_rows = rows if out_rows is None else out_rows
    xb = x_row0 // TM_FFN
    ob = out_row0 // TM_FFN
    in_specs = [
        pl.BlockSpec((TM_FFN, d), lambda i, j: (i + xb, 0)),
        pl.BlockSpec((None, 1, d), lambda i, j: (w_idx, 0, 0)),
        pl.BlockSpec((None, d, TF), lambda i, j: (w_idx, 0, j)),
        pl.BlockSpec((None, d, TF), lambda i, j: (w_idx, 0, j)),
        pl.BlockSpec((None, TF, d), lambda i, j: (w_idx, j, 0)),
    ]
    args = [x, g, wg, wu, wd]
    aliases = {}
    if dest is not None:
        in_specs.append(pl.BlockSpec(memory_space=pl.ANY))
        args.append(dest)
        aliases = {len(args) - 1: 0}
    return pl.pallas_call(
        _ffn_kernel,
        grid=(rows // TM_FFN, fp // TF),
        in_specs=in_specs,
        out_specs=pl.BlockSpec((TM_FFN, d), lambda i, j: (i + ob, 0)),
        out_shape=jax.ShapeDtypeStruct((out_rows, d), F32),
        scratch_shapes=[pltpu.VMEM((TM_FFN, d), BF16)],
        input_output_aliases=aliases,
        compiler_params=_params(("parallel", "arbitrary")),
        name="ffn_half",
    )(*args)


def _proj_kernel(x_ref, g_ref, w_ref, o_ref):
    n = _rms(x_ref[...], g_ref[...]).astype(BF16)
    for c in range(0, w_ref.shape[1], TN):
        o_ref[:, c:c + TN] = jnp.dot(n, w_ref[:, c:c + TN], preferred_element_type=F32).astype(o_ref.dtype)


def _norm_proj(x, g, w, out_dtype, name):
    t, d = x.shape
    n = w.shape[1]
    return pl.pallas_call(
        _proj_kernel,
        grid=(t // TM,),
        in_specs=[
            pl.BlockSpec((TM, d), lambda i: (i, 0)),
            _resident((1, d)),
            _resident((d, n)),
        ],
        out_specs=pl.BlockSpec((TM, n), lambda i: (i, 0)),
        out_shape=jax.ShapeDtypeStruct((t, n), out_dtype),
        compiler_params=_params(("parallel",)),
        name=name,
    )(x, g, w)


def _qkv_kernel(x_ref, g_ref, w_ref, hg_ref, cos_ref, sin_ref, o_ref, *, n_rope):
    n = _rms(x_ref[...], g_ref[...]).astype(BF16)
    u = jnp.dot(n, w_ref[...], preferred_element_type=F32)
    c = cos_ref[...]
    s = sin_ref[...]
    for h in range(u.shape[1] // HEAD_DIM):
        sl = slice(h * HEAD_DIM, (h + 1) * HEAD_DIM)
        uh = u[:, sl]
        if h < n_rope:
            ms = jnp.mean(uh * uh, axis=-1, keepdims=True)
            uh = uh * lax.rsqrt(ms + NORM_EPS) * hg_ref[:, sl]
            uh = uh * c + pltpu.roll(uh, HEAD_DIM // 2, 1) * s
        o_ref[:, sl] = uh.astype(o_ref.dtype)


def _qkv_proj(x, g, w, head_gain, cos, sin, seq_len):
    t, d = x.shape
    n = w.shape[1]
    tiles_per_seq = seq_len // TM
    n_rope = N_Q_HEADS + N_KV_HEADS
    return pl.pallas_call(
        functools.partial(_qkv_kernel, n_rope=n_rope),
        grid=(t // TM,),
        in_specs=[
            pl.BlockSpec((TM, d), lambda i: (i, 0)),
            _resident((1, d)),
            _resident((d, n)),
            _resident((1, n)),
            pl.BlockSpec((TM, HEAD_DIM), lambda i: (i % tiles_per_seq, 0)),
            pl.BlockSpec((TM, HEAD_DIM), lambda i: (i % tiles_per_seq, 0)),
        ],
        out_specs=pl.BlockSpec((TM, n), lambda i: (i, 0)),
        out_shape=jax.ShapeDtypeStruct((t, n), BF16),
        compiler_params=_params(("parallel",)),
        name="qkv_proj",
    )(x, g, w, head_gain, cos, sin)


def _outproj_kernel(*refs, n_y):
    r_ref = refs[0]
    y_refs = refs[1:1 + n_y]
    w_ref = refs[1 + n_y]
    o_ref = refs[2 + n_y]
    ys = [y_ref[...] for y_ref in y_refs]
    for c in range(0, o_ref.shape[1], TN):
        acc = r_ref[:, c:c + TN]
        k0 = 0
        for y in ys:
            k = y.shape[1]
            acc = acc + jnp.dot(y, w_ref[k0:k0 + k, c:c + TN], preferred_element_type=F32)
            k0 += k
        o_ref[:, c:c + TN] = acc


def _out_proj(res, ys, w, name):
    t, d = res.shape
    return pl.pallas_call(
        functools.partial(_outproj_kernel, n_y=len(ys)),
        grid=(t // TM,),
        in_specs=[pl.BlockSpec((TM, d), lambda i: (i, 0))]
        + [pl.BlockSpec((TM, y.shape[1]), lambda i: (i, 0)) for y in ys]
        + [_resident(w.shape)],
        out_specs=pl.BlockSpec((TM, d), lambda i: (i, 0)),
        out_shape=jax.ShapeDtypeStruct((t, d), F32),
        compiler_params=_params(("parallel",)),
        name=name,
    )(res, *ys, w)


def _attn_kernel(q_ref, k_ref, v_ref, o_ref, st_ref):
    seq = k_ref.shape[0]
    n_heads = q_ref.shape[1] // HEAD_DIM
    nt = (((1,), (1,)), ((), ()))
    tn = (((0,), (0,)), ((), ()))
    chunks = [slice(c * ATT_KC, (c + 1) * ATT_KC) for c in range(seq // ATT_KC)]

    def scores(g, rows, slot, m):
        s = lax.dot_general(k_ref[rows, :], q_ref[:, g * HEAD_DIM:(g + 1) * HEAD_DIM], nt,
                            preferred_element_type=F32)
        st_ref[slot, rows, :] = s
        cm = jnp.max(s, axis=0, keepdims=True)
        return cm if m is None else jnp.maximum(m, cm)

    m = None
    for rows in chunks:
        m = scores(0, rows, 0, m)
    for g in range(n_heads):
        cur = g % 2
        acc = l = m_next = None
        for rows in chunks:
            if g + 1 < n_heads:
                m_next = scores(g + 1, rows, 1 - cur, m_next)
            p = jnp.exp2(st_ref[cur, rows, :] - m)
            ps = jnp.sum(p, axis=0, keepdims=True)
            pv = lax.dot_general(v_ref[rows, :], p.astype(BF16), tn, preferred_element_type=F32)
            l = ps if l is None else l + ps
            acc = pv if acc is None else acc + pv
        o_ref[:, g * HEAD_DIM:(g + 1) * HEAD_DIM] = (acc / l).T.astype(o_ref.dtype)
        m = m_next


def _attention(qkv, batch, seq_len, tq):
    t = qkv.shape[0]
    gw = GQA_GROUP * HEAD_DIM
    nq = seq_len // tq
    k_off = N_Q_HEADS
    v_off = N_Q_HEADS + N_KV_HEADS
    return pl.pallas_call(
        _attn_kernel,
        grid=(batch, N_KV_HEADS, nq),
        in_specs=[
            pl.BlockSpec((tq, gw), lambda b, h, i: (b * nq + i, h)),
            pl.BlockSpec((seq_len, HEAD_DIM), lambda b, h, i: (b, k_off + h)),
            pl.BlockSpec((seq_len, HEAD_DIM), lambda b, h, i: (b, v_off + h)),
        ],
        out_specs=pl.BlockSpec((tq, gw), lambda b, h, i: (b * nq + i, h)),
        out_shape=jax.ShapeDtypeStruct((t, N_Q_HEADS * HEAD_DIM), BF16),
        scratch_shapes=[pltpu.VMEM((2, seq_len, tq), F32)],
        compiler_params=_params(("parallel", "parallel", "arbitrary")),
        name="attention",
    )(qkv, qkv, qkv)


def _fill_padded(src_ref, pad_ref):
    seq = src_ref.shape[0]
    zeros = jnp.zeros((SUBLANES, pad_ref.shape[1]), F32)
    pad_ref[0:SUBLANES, :] = zeros
    pad_ref[seq + SUBLANES:seq + 2 * SUBLANES, :] = zeros
    pad_ref[SUBLANES:seq + SUBLANES, :] = src_ref[...]


def _short_conv_rows(pad_ref, w, b, r0):
    n = CONV_ROWS + 2 * SUBLANES
    win = pad_ref[pl.ds(r0, n), :]
    taps = (pltpu.roll(win, 1, 0), win, pltpu.roll(win, n - 1, 0))
    y = b
    for j in range(HY_SHORT_W):
        y = y + taps[j][SUBLANES:SUBLANES + CONV_ROWS, :] * w[j:j + 1, :]
    return y


def _filter_kernel(z_ref, w1_ref, b1_ref, w2_ref, b2_ref, w3_ref, fr_ref, dl_ref, ks_ref, kd_ref, *, seq_len):
    hi = lax.Precision.HIGHEST
    fr = fr_ref[...]
    h = jnp.sin(fr * (jnp.dot(z_ref[...], w1_ref[...], precision=hi, preferred_element_type=F32) + b1_ref[...]))
    for i in range(HY_N_INNER):
        h = jnp.sin(fr * (jnp.dot(h, w2_ref[i], precision=hi, preferred_element_type=F32) + b2_ref[i]))
    k = jnp.dot(h, w3_ref[...], precision=hi, preferred_element_type=F32)
    tl, half = ks_ref.shape
    pos = lax.broadcasted_iota(jnp.int32, (tl, half), 0) + pl.program_id(0) * tl
    tnorm = pos.astype(F32) / (seq_len - 1)
    decay = jnp.exp(-tnorm * dl_ref[...])
    kf = k[:, :half] * decay
    kb = jnp.where(pos == 0, 0.0, k[:, half:] * decay)
    ks_ref[...] = (kf + kb).astype(ks_ref.dtype)
    kd_ref[...] = (kf - kb).astype(kd_ref.dtype)


def _hyena_filter_taps(zfeat, w1, b1, w2, b2, w3, freq, deltas2, seq_len, tl):
    half = deltas2.shape[1]
    return pl.pallas_call(
        functools.partial(_filter_kernel, seq_len=seq_len),
        grid=(seq_len // tl,),
        in_specs=[
            pl.BlockSpec((tl, zfeat.shape[1]), lambda i: (i, 0)),
            _resident(w1.shape),
            _resident(b1.shape),
            _resident(w2.shape),
            _resident(b2.shape),
            _resident(w3.shape),
            _resident(freq.shape),
            _resident(deltas2.shape),
        ],
        out_specs=[pl.BlockSpec((tl, half), lambda i: (i, 0))] * 2,
        out_shape=[jax.ShapeDtypeStruct((seq_len, half), BF16)] * 2,
        compiler_params=_params(("parallel",)),
        name="hyena_filter_taps",
    )(zfeat, w1, b1, w2, b2, w3, freq, deltas2)


def _spectrum_kernel(p_ref, ks_ref, kd_ref, ka_ref, kc_ref, ny_ref, *, period):
    fb = ka_ref.shape[0]
    ks = ks_ref[...]
    re = jnp.dot(p_ref[0:fb, :], ks, preferred_element_type=F32)
    im = jnp.dot(p_ref[fb:2 * fb, :], kd_ref[...], preferred_element_type=F32)
    ny = jnp.dot(p_ref[fb:fb + SUBLANES, :], ks, preferred_element_type=F32)
    is_dc = (lax.broadcasted_iota(jnp.int32, re.shape, 0) + pl.program_id(1) * fb) == 0
    ka_ref[...] = re * jnp.where(is_dc, 1.0 / period, 2.0 / period)
    kc_ref[...] = jnp.where(is_dc, 0.0, im * (2.0 / period))
    ny_ref[...] = ny * (1.0 / period)


def _hyena_spectrum(p, ksum, kdiff, tcs):
    seq_len, cols = ksum.shape
    nf = p.shape[0] // (2 * FB)
    return pl.pallas_call(
        functools.partial(_spectrum_kernel, period=2 * seq_len),
        grid=(cols // tcs, nf),
        in_specs=[
            pl.BlockSpec((2 * FB, seq_len), lambda c, f: (f, 0)),
            pl.BlockSpec((seq_len, tcs), lambda c, f: (0, c)),
            pl.BlockSpec((seq_len, tcs), lambda c, f: (0, c)),
        ],
        out_specs=[
            pl.BlockSpec((FB, tcs), lambda c, f: (f, c)),
            pl.BlockSpec((FB, tcs), lambda c, f: (f, c)),
            pl.BlockSpec((SUBLANES, tcs), lambda c, f: (f, c)),
        ],
        out_shape=[
            jax.ShapeDtypeStruct((nf * FB, cols), F32),
            jax.ShapeDtypeStruct((nf * FB, cols), F32),
            jax.ShapeDtypeStruct((nf * SUBLANES, cols), F32),
        ],
        compiler_params=_params(("parallel", "parallel")),
        name="hyena_filter_spectrum",
    )(p, ksum, kdiff)


def _longconv_kernel(z_ref, gate_ref, swz_ref, sbz_ref, swg_ref, sbg_ref, p_ref, ka_ref, kc_ref, ny_ref,
                     bias_ref, o_ref, pad_ref, zc_ref, gc_ref, zb_ref, acc_ref, *, conv_z):
    f = pl.program_id(2)
    fb = ka_ref.shape[0]
    seq = z_ref.shape[0]

    @pl.when(f == 0)
    def _():
        if conv_z:
            _fill_padded(z_ref, pad_ref)
            w = swz_ref[...]
            b = sbz_ref[...]

            def body(i, carry):
                r0 = pl.multiple_of(i * CONV_ROWS, CONV_ROWS)
                y = _short_conv_rows(pad_ref, w, b, r0)
                zc_ref[pl.ds(r0, CONV_ROWS), :] = y
                zb_ref[pl.ds(r0, CONV_ROWS), :] = y.astype(BF16)
                return carry

            lax.fori_loop(0, seq // CONV_ROWS, body, 0)
        else:
            zb_ref[...] = z_ref[...].astype(BF16)
        acc_ref[...] = jnp.zeros_like(acc_ref)
        _fill_padded(gate_ref, pad_ref)

    wg = swg_ref[...]
    bg = sbg_ref[...]
    for k in range(fb // CONV_ROWS):
        r0 = pl.multiple_of(f * fb, fb) + k * CONV_ROWS
        gc_ref[pl.ds(r0, CONV_ROWS), :] = _short_conv_rows(pad_ref, wg, bg, r0)

    spec = jnp.dot(p_ref[...], zb_ref[...], preferred_element_type=F32)
    xre = spec[:fb]
    xim = spec[fb:]
    ka = ka_ref[...]
    kc = kc_ref[...]
    is_dc = (lax.broadcasted_iota(jnp.int32, ka.shape, 0) + f * fb) == 0
    kd = jnp.where(is_dc, ny_ref[0:1, :], ka)
    yre = (xre * ka - xim * kc).astype(BF16)
    yim = (xre * kc + xim * kd).astype(BF16)
    tn = (((0,), (0,)), ((), ()))
    acc_ref[...] += (lax.dot_general(p_ref[0:fb, :], yre, tn, preferred_element_type=F32)
                     + lax.dot_general(p_ref[fb:2 * fb, :], yim, tn, preferred_element_type=F32))

    @pl.when(f == pl.num_programs(2) - 1)
    def _():
        bias = bias_ref[...]

        def body(i, carry):
            rows = pl.ds(pl.multiple_of(i * CONV_ROWS, CONV_ROWS), CONV_ROWS)
            zsrc = zc_ref if conv_z else z_ref
            o_ref[rows, :] = (gc_ref[rows, :] * (acc_ref[rows, :] + zsrc[rows, :] * bias)).astype(o_ref.dtype)
            return carry

        lax.fori_loop(0, seq // CONV_ROWS, body, 0)


def _long_conv(z, z_col, gate, gate_col, short_w, short_b, p, ka, kc, ny, bias, order, conv_z,
               batch, seq_len, tc, out_dtype):
    c = bias.shape[1] // HY_ORDER
    t = z.shape[0]
    nc = c // tc
    nf = p.shape[0] // (2 * FB)
    zc_col = z_col if conv_z else 0
    return pl.pallas_call(
        functools.partial(_longconv_kernel, conv_z=conv_z),
        grid=(batch, nc, nf),
        in_specs=[
            pl.BlockSpec((seq_len, tc), lambda b, j, f: (b, z_col * nc + j)),
            pl.BlockSpec((seq_len, tc), lambda b, j, f: (b, gate_col * nc + j)),
            pl.BlockSpec((HY_SHORT_W, tc), lambda b, j, f: (0, zc_col * nc + j)),
            pl.BlockSpec((1, tc), lambda b, j, f: (0, zc_col * nc + j)),
            pl.BlockSpec((HY_SHORT_W, tc), lambda b, j, f: (0, gate_col * nc + j)),
            pl.BlockSpec((1, tc), lambda b, j, f: (0, gate_col * nc + j)),
            pl.BlockSpec((2 * FB, seq_len), lambda b, j, f: (f, 0)),
            pl.BlockSpec((FB, tc), lambda b, j, f: (f, order * nc + j)),
            pl.BlockSpec((FB, tc), lambda b, j, f: (f, order * nc + j)),
            pl.BlockSpec((SUBLANES, tc), lambda b, j, f: (0, order * nc + j)),
            pl.BlockSpec((1, tc), lambda b, j, f: (0, order * nc + j)),
        ],
        out_specs=pl.BlockSpec((seq_len, tc), lambda b, j, f: (b, j)),
        out_shape=jax.ShapeDtypeStruct((t, c), out_dtype),
        scratch_shapes=[
            pltpu.VMEM((seq_len + 2 * SUBLANES, tc), F32),
            pltpu.VMEM((seq_len, tc) if conv_z else (SUBLANES, LANES), F32),
            pltpu.VMEM((seq_len, tc), F32),
            pltpu.VMEM((seq_len, tc), BF16),
            pltpu.VMEM((seq_len, tc), F32),
        ],
        compiler_params=_params(("parallel", "parallel", "arbitrary")),
        name="hyena_long_conv",
    )(z, gate, short_w, short_b, short_w, short_b, p, ka, kc, ny, bias)


def _dft_matrix(seq_len):
    period = 2 * seq_len
    split = 64
    f = jnp.arange(seq_len, dtype=jnp.int32)[:, None]

    def trig(tvals):
        ang = ((f * tvals[None, :]) % period).astype(F32) * (2.0 * math.pi / period)
        return jnp.cos(ang), jnp.sin(ang)

    ca, sa = trig(jnp.arange(seq_len // split, dtype=jnp.int32) * split)
    cb, sb = trig(jnp.arange(split, dtype=jnp.int32))
    re = (ca[:, :, None] * cb[:, None, :] - sa[:, :, None] * sb[:, None, :]).reshape(seq_len, seq_len)
    im = -(sa[:, :, None] * cb[:, None, :] + ca[:, :, None] * sb[:, None, :]).reshape(seq_len, seq_len)
    nyq = jnp.where(jnp.arange(seq_len) % 2 == 0, 1.0, -1.0).astype(F32)
    im = im.at[0].set(nyq)
    nf = seq_len // FB
    p = jnp.stack([re.reshape(nf, FB, seq_len), im.reshape(nf, FB, seq_len)], axis=1)
    return p.reshape(2 * seq_len, seq_len).astype(BF16)


def _position_features(seq_len, width):
    pos = jnp.arange(seq_len, dtype=F32)
    t = pos / (seq_len - 1)
    w = 2.0 * math.pi * pos / seq_len
    f = jnp.linspace(1e-4, HY_BANDS - 1, HY_BANDS, dtype=F32)
    fw = w[:, None] * f[None, :]
    z = jnp.concatenate([t[:, None], jnp.cos(fw), -jnp.sin(fw)], axis=-1)
    return jnp.pad(z, ((0, 0), (0, width - z.shape[1])))


def _softplus(x):
    return jnp.maximum(x, 0.0) + jnp.log1p(jnp.exp(-jnp.abs(x)))


def _gelu_tanh(x):
    return 0.5 * x * (1.0 + jnp.tanh(math.sqrt(2.0 / math.pi) * (x + 0.044715 * (x * x * x))))


def _rglru_kernel(g_ref, x_ref, cw_ref, cb_ref, w_ref, bias_ref, lam_ref, o_ref,
                  xp_ref, xq_ref, a_ref, b_ref, hl_ref, al_ref, hs_ref):
    seq, width = x_ref.shape
    nck = SUBLANES
    chunk = seq // nck
    pitch = chunk + SCAN_PITCH_PAD
    slabs = width // LANES
    halo = RNN_CONV_W - 1

    for c in range(nck):
        for sl in range(slabs):
            xp_ref[sl, c * pitch:c * pitch + chunk, :] = x_ref[c * chunk:(c + 1) * chunk, sl * LANES:(sl + 1) * LANES]

    def to_chunk_major(i, carry):
        s0 = i * SCAN_GROUP
        row0 = pl.multiple_of((s0 + halo) * nck, nck)
        for k in range(SCAN_GROUP):
            for sl in range(slabs):
                xq_ref[pl.ds(row0 + k * nck, nck), sl * LANES:(sl + 1) * LANES] = (
                    xp_ref[sl, pl.ds(s0 + k, nck, stride=pitch), :])
        return carry

    lax.fori_loop(0, chunk // SCAN_GROUP, to_chunk_major, 0)

    sub = lax.broadcasted_iota(jnp.int32, (nck, width), 0)
    for k in range(halo):
        tail = xq_ref[(chunk + k) * nck:(chunk + k + 1) * nck, :]
        xq_ref[k * nck:(k + 1) * nck, :] = jnp.where(sub == 0, 0.0, pltpu.roll(tail, 1, 0))
        head = xq_ref[(halo + k) * nck:(halo + k + 1) * nck, :]
        xq_ref[(halo + chunk + k) * nck:(halo + chunk + k + 1) * nck, :] = jnp.where(
            sub == nck - 1, 0.0, pltpu.roll(head, nck - 1, 0))

    for d in range(2):
        ch = (0.5 * RNN_C) * _softplus(-lam_ref[d:d + 1, :])
        reset_row = 0 if d == 0 else seq - 1

        def gates(bi, carry):
            base = pl.multiple_of(bi * GATE_ROWS, GATE_ROWS)
            xc = cb_ref[d:d + 1, :]
            for j in range(RNN_CONV_W):
                step = j if d == 0 else 2 * halo - j
                xc = xc + xq_ref[pl.ds(base + step * nck, GATE_ROWS), :] * cw_ref[d, j:j + 1, :]
            gt = jnp.dot(xc.astype(BF16), w_ref[d, 0], preferred_element_type=F32) + bias_ref[d, 0]
            tr = jnp.tanh(gt[:, :width])
            ti = jnp.tanh(gt[:, width:])
            nla = tr * ch + ch
            a = jnp.exp2(nla * (-LOG2E))
            m2 = jnp.tanh(nla) * (a * a + 1.0)
            mult = jnp.where(m2 > 0.0, m2 * lax.rsqrt(m2), 0.0)
            ixc = (0.5 * ti + 0.5) * xc
            bt = mult * ixc
            a_ref[d, pl.ds(base, GATE_ROWS), :] = a
            b_ref[d, pl.ds(base, GATE_ROWS), :] = bt

            @pl.when(bi == reset_row // GATE_ROWS)
            def _():
                lo = (reset_row % GATE_ROWS) // nck * nck
                fixed = jnp.where(sub == reset_row % nck, ixc[lo:lo + nck, :], bt[lo:lo + nck, :])
                b_ref[d, pl.ds(base + lo, nck), :] = fixed

            return carry

        lax.fori_loop(0, seq // GATE_ROWS, gates, 0)

    def scan(i, carry):
        state = list(carry)
        group_rows = SCAN_GROUP * nck
        for k in range(SCAN_GROUP):
            for d in range(2):
                h, acc = state[2 * d], state[2 * d + 1]
                if d == 0:
                    row = pl.multiple_of(i * group_rows, nck) + k * nck
                else:
                    row = pl.multiple_of((chunk - SCAN_GROUP) * nck - i * group_rows, nck) + (SCAN_GROUP - 1 - k) * nck
                av = a_ref[d, pl.ds(row, nck), :]
                h = av * h + b_ref[d, pl.ds(row, nck), :]
                acc = acc * av
                hl_ref[d, pl.ds(row, nck), :] = h
                al_ref[d, pl.ds(row, nck), :] = acc
                state[2 * d], state[2 * d + 1] = h, acc
        return tuple(state)

    init = (jnp.zeros((nck, width), F32), jnp.ones((nck, width), F32)) * 2
    ends = lax.fori_loop(0, chunk // SCAN_GROUP, scan, init)

    carries = []
    for d in range(2):
        h_end, a_end = ends[2 * d], ends[2 * d + 1]
        rows = [None] * nck
        cur = jnp.zeros((1, width), F32)
        for c in (range(nck) if d == 0 else range(nck - 1, -1, -1)):
            rows[c] = cur
            cur = a_end[c:c + 1, :] * cur + h_end[c:c + 1, :]
        carries.append(jnp.concatenate(rows, axis=0))

    def fold(i, carry):
        s0 = i * SCAN_GROUP
        row0 = pl.multiple_of(s0 * nck, nck)
        for k in range(SCAN_GROUP):
            rows = pl.ds(row0 + k * nck, nck)
            hv = (hl_ref[0, rows, :] + al_ref[0, rows, :] * carries[0]
                  + hl_ref[1, rows, :] + al_ref[1, rows, :] * carries[1])
            for sl in range(slabs):
                hs_ref[sl, pl.ds(s0 + k, nck, stride=pitch), :] = hv[:, sl * LANES:(sl + 1) * LANES]
        return carry

    lax.fori_loop(0, chunk // SCAN_GROUP, fold, 0)

    def finish(c, carry):
        r0 = pl.multiple_of(c * chunk, chunk)
        p0 = pl.multiple_of(c * pitch, SUBLANES)
        gate = _gelu_tanh(g_ref[pl.ds(r0, chunk), :])
        for sl in range(slabs):
            o_ref[pl.ds(r0, chunk), sl * LANES:(sl + 1) * LANES] = (
                hs_ref[sl, pl.ds(p0, chunk), :] * gate[:, sl * LANES:(sl + 1) * LANES]).astype(o_ref.dtype)
        return carry

    lax.fori_loop(0, nck, finish, 0)


def _rglru(u, conv_w, conv_b, w_gates, b_gates, lam, batch, seq_len):
    t, two_d = u.shape
    d = two_d // 2
    width = d // RNN_HEADS
    slabs = width // LANES
    chunk = seq_len // SUBLANES
    pitched = SUBLANES * (chunk + SCAN_PITCH_PAD)
    halo_rows = 2 * (RNN_CONV_W - 1) * SUBLANES
    return pl.pallas_call(
        _rglru_kernel,
        grid=(batch, RNN_HEADS),
        in_specs=[
            pl.BlockSpec((seq_len, width), lambda b, h: (b, h)),
            pl.BlockSpec((seq_len, width), lambda b, h: (b, RNN_HEADS + h)),
            pl.BlockSpec((2, RNN_CONV_W, width), lambda b, h: (0, 0, h)),
            pl.BlockSpec((2, width), lambda b, h: (0, h)),
            pl.BlockSpec((2, 1, width, 2 * width), lambda b, h: (0, h, 0, 0)),
            pl.BlockSpec((2, 1, 1, 2 * width), lambda b, h: (0, h, 0, 0)),
            pl.BlockSpec((2, width), lambda b, h: (0, h)),
        ],
        out_specs=pl.BlockSpec((seq_len, width), lambda b, h: (b, h)),
        out_shape=jax.ShapeDtypeStruct((t, d), BF16),
        scratch_shapes=[
            pltpu.VMEM((slabs, pitched, LANES), F32),
            pltpu.VMEM((seq_len + halo_rows, width), F32),
            pltpu.VMEM((2, seq_len, width), F32),
            pltpu.VMEM((2, seq_len, width), F32),
            pltpu.VMEM((2, seq_len, width), F32),
            pltpu.VMEM((2, seq_len, width), F32),
            pltpu.VMEM((slabs, pitched, LANES), F32),
        ],
        compiler_params=_params(("parallel", "parallel")),
        name="rglru",
    )(u, u, conv_w, conv_b, w_gates, b_gates, lam)


def _deinterleave_heads(w, n_heads):
    lead = w.shape[:-1]
    w = w.reshape(lead + (n_heads, HEAD_DIM // 2, 2))
    w = jnp.swapaxes(w, -1, -2)
    return w.reshape(lead + (n_heads * HEAD_DIM,))


def _rope_tables(seq_len):
    rows = seq_len // GRID_W
    row = jnp.repeat(jnp.arange(rows, dtype=F32), GRID_W)
    col = jnp.tile(jnp.arange(GRID_W, dtype=F32), rows)
    axis_dim = HEAD_DIM // 2
    omega = ROPE_THETA ** (-jnp.arange(0, axis_dim, 2, dtype=F32) / axis_dim)
    ang = jnp.concatenate([row[:, None] * omega[None], col[:, None] * omega[None]], axis=-1)
    c, s = jnp.cos(ang), jnp.sin(ang)
    return jnp.concatenate([c, c], axis=-1), jnp.concatenate([-s, s], axis=-1)


def _even_mixer(x, batch, seq_len, norm_g, w_in, short_w, short_b, f_w1, f_b1, f_w2, f_b2, f_w3, f_freq,
                hy_bias, q_g, k_g, w_out):
    d = x.shape[1]
    hy_d = hy_bias.shape[1]
    s0 = (HY_ORDER + 1) * hy_d
    nq = N_Q_HEADS * HEAD_DIM
    nk = N_KV_HEADS * HEAD_DIM
    g = norm_g.reshape(1, d)

    u_hy = _norm_proj(x, g, w_in[:, :s0].astype(BF16), F32, "hyena_in_proj")

    hid = LANES
    hpad = hid - f_w1.shape[1]
    zfeat = _position_features(seq_len, LANES)
    w1 = jnp.pad(f_w1, ((0, LANES - f_w1.shape[0]), (0, hpad)))
    b1 = jnp.pad(f_b1, (0, hpad)).reshape(1, hid)
    w2 = jnp.pad(f_w2, ((0, 0), (0, hpad), (0, hpad)))
    b2 = jnp.pad(f_b2, ((0, 0), (0, hpad))).reshape(HY_N_INNER, 1, hid)
    w3 = jnp.pad(f_w3, ((0, hpad), (0, 0)))
    freq = jnp.pad(f_freq, (0, hpad)).reshape(1, hid)
    max_decay = math.log(HY_DECAY_TARGET) / HY_FAST_DECAY
    min_decay = math.log(HY_DECAY_TARGET) / HY_SLOW_DECAY
    deltas = jnp.abs(jnp.linspace(min_decay, max_decay, hy_d, dtype=F32))
    deltas2 = jnp.tile(deltas, HY_ORDER).reshape(1, HY_ORDER * hy_d)
    ksum, kdiff = _hyena_filter_taps(zfeat, w1, b1, w2, b2, w3, freq, deltas2, seq_len, 256)
    p = _dft_matrix(seq_len)
    ka, kc, ny = _hyena_spectrum(p, ksum, kdiff, 512)
    bias = hy_bias.reshape(1, HY_ORDER * hy_d)
    sb = short_b.reshape(1, s0)
    z1 = _long_conv(u_hy, 0, u_hy, 1, short_w, sb, p, ka, kc, ny, bias, 0, True, batch, seq_len, 512, F32)
    y_hy = _long_conv(z1, 0, u_hy, 2, short_w, sb, p, ka, kc, ny, bias, 1, False, batch, seq_len, 512, BF16)

    w_qkv = jnp.concatenate([
        _deinterleave_heads(w_in[:, s0:s0 + nq], N_Q_HEADS),
        _deinterleave_heads(w_in[:, s0 + nq:s0 + nq + nk], N_KV_HEADS),
        w_in[:, s0 + nq + nk:]], axis=1).astype(BF16)
    qg = _deinterleave_heads(q_g, 1) * (HEAD_DIM ** -0.5 * LOG2E)
    kg = _deinterleave_heads(k_g, 1)
    head_gain = jnp.concatenate([jnp.tile(qg, N_Q_HEADS), jnp.tile(kg, N_KV_HEADS),
                                 jnp.ones((nk,), F32)]).reshape(1, nq + 2 * nk)
    cos, sin = _rope_tables(seq_len)
    qkv = _qkv_proj(x, g, w_qkv, head_gain, cos, sin, seq_len)
    y_at = _attention(qkv, batch, seq_len, 512)

    return _out_proj(x, [y_hy, y_at], w_out.astype(BF16), "even_out_proj")


def _odd_mixer(x, batch, seq_len, norm_g, w_in, conv_w, conv_b, wa, ba, wx, bx, lam, w_out):
    d = x.shape[1]
    u = _norm_proj(x, norm_g.reshape(1, d), w_in.astype(BF16), F32, "rglru_in_proj")
    w_gates = (0.5 * jnp.concatenate([wa, wx], axis=-1)).astype(BF16)
    b_gates = 0.5 * jnp.concatenate([ba, bx], axis=-1)[:, :, None, :]
    y = _rglru(u, conv_w, conv_b, w_gates, b_gates, lam, batch, seq_len)
    return _out_proj(x, [y], w_out.astype(BF16), "rglru_out_proj")


def kernel(x_prompt, x_sample, ffn_norm, ffn_w_gate, ffn_w_up, ffn_w_down, even_norm, even_w_in, hy_short_w, hy_short_b, hy_filt_w1, hy_filt_b1, hy_filt_w2, hy_filt_b2, hy_filt_w3, hy_filt_freq, hy_bias, q_norm, k_norm, even_w_out, odd_norm, odd_w_in, rg_conv_w, rg_conv_b, rg_wa, rg_ba, rg_wx, rg_bx, rg_lambda, odd_w_out):
    bp, seq_len, d = x_prompt.shape
    bs = x_sample.shape[0]
    assert x_sample.shape[1:] == (seq_len, d)
    batch = bp + bs
    rows_p = bp * seq_len

    depth = ffn_norm.shape[0]
    ff = ffn_w_gate.shape[-1]
    fp = -(-ff // TF) * TF
    wg = _cast_pad(ffn_w_gate.reshape(depth * 2, d, ff), d, fp, 1024, TF)
    wu = _cast_pad(ffn_w_up.reshape(depth * 2, d, ff), d, fp, 1024, TF)
    wd = _cast_pad(ffn_w_down.reshape(depth * 2, ff, d), fp, d, TF, d)
    fg = ffn_norm.reshape(depth * 2, 1, d)
    rows_s = bs * seq_len
    total = rows_p + rows_s

    x = None
    for layer in range(depth):
        if layer == 0:
            x = _ffn_half(x_prompt.reshape(rows_p, d), fg, 0, wg, wu, wd, out_rows=total)
            x = _ffn_half(x_sample.reshape(rows_s, d), fg, 0, wg, wu, wd, out_rows=total, out_row0=rows_p, dest=x)
        else:
            x = _ffn_half(x, fg, 2 * layer, wg, wu, wd)
        j = layer // 2
        if layer % 2 == 0:
            x = _even_mixer(x, batch, seq_len, even_norm[j], even_w_in[j], hy_short_w[j], hy_short_b[j],
                            hy_filt_w1[j], hy_filt_b1[j], hy_filt_w2[j], hy_filt_b2[j], hy_filt_w3[j],
                            hy_filt_freq[j], hy_bias[j], q_norm[j], k_norm[j], even_w_out[j])
        else:
            x = _odd_mixer(x, batch, seq_len, odd_norm[j], odd_w_in[j], rg_conv_w[j], rg_conv_b[j],
                           rg_wa[j], rg_ba[j], rg_wx[j], rg_bx[j], rg_lambda[j], odd_w_out[j])
        if layer == depth - 1:
            y_prompt = _ffn_half(x, fg, 2 * layer + 1, wg, wu, wd, rows=rows_p)
            y_sample = _ffn_half(x, fg, 2 * layer + 1, wg, wu, wd, rows=rows_s, x_row0=rows_p)
        else:
            x = _ffn_half(x, fg, 2 * layer + 1, wg, wu, wd)

    return (y_prompt.reshape(bp, seq_len, d), y_sample.reshape(bs, seq_len, d))
```

```python
import functools
import math

import jax
import jax.numpy as jnp
from jax import lax
from jax.experimental import pallas as pl
from jax.experimental.pallas import tpu as pltpu

F32 = jnp.float32
BF16 = jnp.bfloat16

NORM_EPS = 1e-6
GRID_W = 64
HEAD_DIM = 128
N_Q_HEADS = 8
N_KV_HEADS = 2
GQA_GROUP = N_Q_HEADS // N_KV_HEADS
ROPE_THETA = 10000.0
HY_ORDER = 2
HY_SHORT_W = 3
HY_EMB_DIM = 33
HY_BANDS = (HY_EMB_DIM - 1) // 2
HY_N_INNER = 2
HY_FAST_DECAY = 0.3
HY_SLOW_DECAY = 1.5
HY_DECAY_TARGET = 1e-2
RNN_HEADS = 8
RNN_CONV_W = 4
RNN_C = 8.0
LOG2E = 1.4426950408889634

LANES = 128
SUBLANES = 8
VMEM_LIMIT = 56 * 1024 * 1024

TM = 512
TM_FFN = 1024
TF = 512
TN = 1024
ATT_KC = 256
FB = 256
CONV_ROWS = 64
GATE_ROWS = 512
SCAN_GROUP = 8
SCAN_PITCH_PAD = 8


def _params(sem, vmem=VMEM_LIMIT):
    return pltpu.CompilerParams(dimension_semantics=sem, vmem_limit_bytes=vmem)


def _resident(shape):
    return pl.BlockSpec(shape, lambda *_: (0,) * len(shape), pipeline_mode=pl.Buffered(1))


def _rms(x, g):
    ms = jnp.mean(x * x, axis=-1, keepdims=True)
    return x * lax.rsqrt(ms + NORM_EPS) * g


def _cast_pad_kernel(w_ref, o_ref, *, rows, cols):
    br, bc = o_ref.shape
    r = lax.broadcasted_iota(jnp.int32, (br, bc), 0) + pl.program_id(1) * br
    c = lax.broadcasted_iota(jnp.int32, (br, bc), 1) + pl.program_id(2) * bc
    w = jnp.where(r < rows, jnp.where(c < cols, w_ref[...], 0.0), 0.0)
    o_ref[...] = w.astype(o_ref.dtype)


def _cast_pad(w, rows_out, cols_out, br, bc):
    n, rows, cols = w.shape
    return pl.pallas_call(
        functools.partial(_cast_pad_kernel, rows=rows, cols=cols),
        grid=(n, rows_out // br, cols_out // bc),
        in_specs=[pl.BlockSpec((None, br, bc), lambda k, i, j: (k, i, j))],
        out_specs=pl.BlockSpec((None, br, bc), lambda k, i, j: (k, i, j)),
        out_shape=jax.ShapeDtypeStruct((n, rows_out, cols_out), BF16),
        compiler_params=_params(("parallel", "parallel", "parallel")),
        name="cast_pad",
    )(w)


def _ffn_kernel(x_ref, g_ref, wg_ref, wu_ref, wd_ref, *rest):
    o_ref, n_ref = rest[-2:]

    @pl.when(pl.program_id(1) == 0)
    def _():
        x = x_ref[...]
        n_ref[...] = _rms(x, g_ref[...]).astype(BF16)
        o_ref[...] = x

    n = n_ref[...]
    hg = jnp.dot(n, wg_ref[...], preferred_element_type=F32)
    hu = jnp.dot(n, wu_ref[...], preferred_element_type=F32)
    h = ((hg * jax.nn.sigmoid(hg)) * (hu * 0.5)).astype(BF16)
    o_ref[...] += jnp.dot(h, wd_ref[...], preferred_element_type=F32)


def _ffn_half(x, g, w_idx, wg, wu, wd, *, rows=None, x_row0=0, out_rows=None, out_row0=0, dest=None):
    d = x.shape[1]
    fp = wg.shape[2]
    rows = x.shape[0] if rows is None else rows
    out_rows = rows if out_rows is None else out_rows
    xb = x_row0 // TM_FFN
    ob = out_row0 // TM_FFN
    in_specs = [
        pl.BlockSpec((TM_FFN, d), lambda i, j: (i + xb, 0)),
        pl.BlockSpec((None, 1, d), lambda i, j: (w_idx, 0, 0)),
        pl.BlockSpec((None, d, TF), lambda i, j: (w_idx, 0, j)),
        pl.BlockSpec((None, d, TF), lambda i, j: (w_idx, 0, j)),
        pl.BlockSpec((None, TF, d), lambda i, j: (w_idx, j, 0)),
    ]
    args = [x, g, wg, wu, wd]
    aliases = {}
    if dest is not None:
        in_specs.append(pl.BlockSpec(memory_space=pl.ANY))
        args.append(dest)
        aliases = {len(args) - 1: 0}
    return pl.pallas_call(
        _ffn_kernel,
        grid=(rows // TM_FFN, fp // TF),
        in_specs=in_specs,
        out_specs=pl.BlockSpec((TM_FFN, d), lambda i, j: (i + ob, 0)),
        out_shape=jax.ShapeDtypeStruct((out_rows, d), F32),
        scratch_shapes=[pltpu.VMEM((TM_FFN, d), BF16)],
        input_output_aliases=aliases,
        compiler_params=_params(("parallel", "arbitrary")),
        name="ffn_half",
    )(*args)


def _proj_kernel(x_ref, g_ref, w_ref, o_ref):
    n = _rms(x_ref[...], g_ref[...]).astype(BF16)
    for c in range(0, w_ref.shape[1], TN):
        o_ref[:, c:c + TN] = jnp.dot(n, w_ref[:, c:c + TN], preferred_element_type=F32).astype(o_ref.dtype)


def _norm_proj(x, g, w, out_dtype, name):
    t, d = x.shape
    n = w.shape[1]
    return pl.pallas_call(
        _proj_kernel,
        grid=(t // TM,),
        in_specs=[
            pl.BlockSpec((TM, d), lambda i: (i, 0)),
            _resident((1, d)),
            _resident((d, n)),
        ],
        out_specs=pl.BlockSpec((TM, n), lambda i: (i, 0)),
        out_shape=jax.ShapeDtypeStruct((t, n), out_dtype),
        compiler_params=_params(("parallel",)),
        name=name,
    )(x, g, w)


def _qkv_kernel(x_ref, g_ref, w_ref, hg_ref, cos_ref, sin_ref, o_ref, *, n_rope):
    n = _rms(x_ref[...], g_ref[...]).astype(BF16)
    u = jnp.dot(n, w_ref[...], preferred_element_type=F32)
    c = cos_ref[...]
    s = sin_ref[...]
    for h in range(u.shape[1] // HEAD_DIM):
        sl = slice(h * HEAD_DIM, (h + 1) * HEAD_DIM)
        uh = u[:, sl]
        if h < n_rope:
            ms = jnp.mean(uh * uh, axis=-1, keepdims=True)
            uh = uh * lax.rsqrt(ms + NORM_EPS) * hg_ref[:, sl]
            uh = uh * c + pltpu.roll(uh, HEAD_DIM // 2, 1) * s
        o_ref[:, sl] = uh.astype(o_ref.dtype)


def _qkv_proj(x, g, w, head_gain, cos, sin, seq_len):
    t, d = x.shape
    n = w.shape[1]
    tiles_per_seq = seq_len // TM
    n_rope = N_Q_HEADS + N_KV_HEADS
    return pl.pallas_call(
        functools.partial(_qkv_kernel, n_rope=n_rope),
        grid=(t // TM,),
        in_specs=[
            pl.BlockSpec((TM, d), lambda i: (i, 0)),
            _resident((1, d)),
            _resident((d, n)),
            _resident((1, n)),
            pl.BlockSpec((TM, HEAD_DIM), lambda i: (i % tiles_per_seq, 0)),
            pl.BlockSpec((TM, HEAD_DIM), lambda i: (i % tiles_per_seq, 0)),
        ],
        out_specs=pl.BlockSpec((TM, n), lambda i: (i, 0)),
        out_shape=jax.ShapeDtypeStruct((t, n), BF16),
        compiler_params=_params(("parallel",)),
        name="qkv_proj",
    )(x, g, w, head_gain, cos, sin)


def _outproj_kernel(*refs, n_y):
    r_ref = refs[0]
    y_refs = refs[1:1 + n_y]
    w_ref = refs[1 + n_y]
    o_ref = refs[2 + n_y]
    ys = [y_ref[...] for y_ref in y_refs]
    for c in range(0, o_ref.shape[1], TN):
        acc = r_ref[:, c:c + TN]
        k0 = 0
        for y in ys:
            k = y.shape[1]
            acc = acc + jnp.dot(y, w_ref[k0:k0 + k, c:c + TN], preferred_element_type=F32)
            k0 += k
        o_ref[:, c:c + TN] = acc


def _out_proj(res, ys, w, name):
    t, d = res.shape
    return pl.pallas_call(
        functools.partial(_outproj_kernel, n_y=len(ys)),
        grid=(t // TM,),
        in_specs=[pl.BlockSpec((TM, d), lambda i: (i, 0))]
        + [pl.BlockSpec((TM, y.shape[1]), lambda i: (i, 0)) for y in ys]
        + [_resident(w.shape)],
        out_specs=pl.BlockSpec((TM, d), lambda i: (i, 0)),
        out_shape=jax.ShapeDtypeStruct((t, d), F32),
        compiler_params=_params(("parallel",)),
        name=name,
    )(res, *ys, w)


def _attn_kernel(q_ref, k_ref, v_ref, o_ref, st_ref):
    seq = k_ref.shape[0]
    n_heads = q_ref.shape[1] // HEAD_DIM
    nt = (((1,), (1,)), ((), ()))
    tn = (((0,), (0,)), ((), ()))
    chunks = [slice(c * ATT_KC, (c + 1) * ATT_KC) for c in range(seq // ATT_KC)]

    def scores(g, rows, slot, m):
        s = lax.dot_general(k_ref[rows, :], q_ref[:, g * HEAD_DIM:(g + 1) * HEAD_DIM], nt,
                            preferred_element_type=F32)
        st_ref[slot, rows, :] = s
        cm = jnp.max(s, axis=0, keepdims=True)
        return cm if m is None else jnp.maximum(m, cm)

    m = None
    for rows in chunks:
        m = scores(0, rows, 0, m)
    for g in range(n_heads):
        cur = g % 2
        acc = l = m_next = None
        for rows in chunks:
            if g + 1 < n_heads:
                m_next = scores(g + 1, rows, 1 - cur, m_next)
            p = jnp.exp2(st_ref[cur, rows, :] - m)
            ps = jnp.sum(p, axis=0, keepdims=True)
            pv = lax.dot_general(v_ref[rows, :], p.astype(BF16), tn, preferred_element_type=F32)
            l = ps if l is None else l + ps
            acc = pv if acc is None else acc + pv
        o_ref[:, g * HEAD_DIM:(g + 1) * HEAD_DIM] = (acc / l).T.astype(o_ref.dtype)
        m = m_next


def _attention(qkv, batch, seq_len, tq):
    t = qkv.shape[0]
    gw = GQA_GROUP * HEAD_DIM
    nq = seq_len // tq
    k_off = N_Q_HEADS
    v_off = N_Q_HEADS + N_KV_HEADS
    return pl.pallas_call(
        _attn_kernel,
        grid=(batch, N_KV_HEADS, nq),
        in_specs=[
            pl.BlockSpec((tq, gw), lambda b, h, i: (b * nq + i, h)),
            pl.BlockSpec((seq_len, HEAD_DIM), lambda b, h, i: (b, k_off + h)),
            pl.BlockSpec((seq_len, HEAD_DIM), lambda b, h, i: (b, v_off + h)),
        ],
        out_specs=pl.BlockSpec((tq, gw), lambda b, h, i: (b * nq + i, h)),
        out_shape=jax.ShapeDtypeStruct((t, N_Q_HEADS * HEAD_DIM), BF16),
        scratch_shapes=[pltpu.VMEM((2, seq_len, tq), F32)],
        compiler_params=_params(("parallel", "parallel", "arbitrary")),
        name="attention",
    )(qkv, qkv, qkv)


def _fill_padded(src_ref, pad_ref):
    seq = src_ref.shape[0]
    zeros = jnp.zeros((SUBLANES, pad_ref.shape[1]), F32)
    pad_ref[0:SUBLANES, :] = zeros
    pad_ref[seq + SUBLANES:seq + 2 * SUBLANES, :] = zeros
    pad_ref[SUBLANES:seq + SUBLANES, :] = src_ref[...]


def _short_conv_rows(pad_ref, w, b, r0):
    n = CONV_ROWS + 2 * SUBLANES
    win = pad_ref[pl.ds(r0, n), :]
    taps = (pltpu.roll(win, 1, 0), win, pltpu.roll(win, n - 1, 0))
    y = b
    for j in range(HY_SHORT_W):
        y = y + taps[j][SUBLANES:SUBLANES + CONV_ROWS, :] * w[j:j + 1, :]
    return y


def _filter_kernel(z_ref, w1_ref, b1_ref, w2_ref, b2_ref, w3_ref, fr_ref, dl_ref, ks_ref, kd_ref, *, seq_len):
    hi = lax.Precision.HIGHEST
    fr = fr_ref[...]
    h = jnp.sin(fr * (jnp.dot(z_ref[...], w1_ref[...], precision=hi, preferred_element_type=F32) + b1_ref[...]))
    for i in range(HY_N_INNER):
        h = jnp.sin(fr * (jnp.dot(h, w2_ref[i], precision=hi, preferred_element_type=F32) + b2_ref[i]))
    k = jnp.dot(h, w3_ref[...], precision=hi, preferred_element_type=F32)
    tl, half = ks_ref.shape
    pos = lax.broadcasted_iota(jnp.int32, (tl, half), 0) + pl.program_id(0) * tl
    tnorm = pos.astype(F32) / (seq_len - 1)
    decay = jnp.exp(-tnorm * dl_ref[...])
    kf = k[:, :half] * decay
    kb = jnp.where(pos == 0, 0.0, k[:, half:] * decay)
    ks_ref[...] = (kf + kb).astype(ks_ref.dtype)
    kd_ref[...] = (kf - kb).astype(kd_ref.dtype)


def _hyena_filter_taps(zfeat, w1, b1, w2, b2, w3, freq, deltas2, seq_len, tl):
    half = deltas2.shape[1]
    return pl.pallas_call(
        functools.partial(_filter_kernel, seq_len=seq_len),
        grid=(seq_len // tl,),
        in_specs=[
            pl.BlockSpec((tl, zfeat.shape[1]), lambda i: (i, 0)),
            _resident(w1.shape),
            _resident(b1.shape),
            _resident(w2.shape),
            _resident(b2.shape),
            _resident(w3.shape),
            _resident(freq.shape),
            _resident(deltas2.shape),
        ],
        out_specs=[pl.BlockSpec((tl, half), lambda i: (i, 0))] * 2,
        out_shape=[jax.ShapeDtypeStruct((seq_len, half), BF16)] * 2,
        compiler_params=_params(("parallel",)),
        name="hyena_filter_taps",
    )(zfeat, w1, b1, w2, b2, w3, freq, deltas2)


def _spectrum_kernel(p_ref, ks_ref, kd_ref, ka_ref, kc_ref, ny_ref, *, period):
    fb = ka_ref.shape[0]
    ks = ks_ref[...]
    re = jnp.dot(p_ref[0:fb, :], ks, preferred_element_type=F32)
    im = jnp.dot(p_ref[fb:2 * fb, :], kd_ref[...], preferred_element_type=F32)
    ny = jnp.dot(p_ref[fb:fb + SUBLANES, :], ks, preferred_element_type=F32)
    is_dc = (lax.broadcasted_iota(jnp.int32, re.shape, 0) + pl.program_id(1) * fb) == 0
    ka_ref[...] = re * jnp.where(is_dc, 1.0 / period, 2.0 / period)
    kc_ref[...] = jnp.where(is_dc, 0.0, im * (2.0 / period))
    ny_ref[...] = ny * (1.0 / period)


def _hyena_spectrum(p, ksum, kdiff, tcs):
    seq_len, cols = ksum.shape
    nf = p.shape[0] // (2 * FB)
    return pl.pallas_call(
        functools.partial(_spectrum_kernel, period=2 * seq_len),
        grid=(cols // tcs, nf),
        in_specs=[
            pl.BlockSpec((2 * FB, seq_len), lambda c, f: (f, 0)),
            pl.BlockSpec((seq_len, tcs), lambda c, f: (0, c)),
            pl.BlockSpec((seq_len, tcs), lambda c, f: (0, c)),
        ],
        out_specs=[
            pl.BlockSpec((FB, tcs), lambda c, f: (f, c)),
            pl.BlockSpec((FB, tcs), lambda c, f: (f, c)),
            pl.BlockSpec((SUBLANES, tcs), lambda c, f: (f, c)),
        ],
        out_shape=[
            jax.ShapeDtypeStruct((nf * FB, cols), F32),
            jax.ShapeDtypeStruct((nf * FB, cols), F32),
            jax.ShapeDtypeStruct((nf * SUBLANES, cols), F32),
        ],
        compiler_params=_params(("parallel", "parallel")),
        name="hyena_filter_spectrum",
    )(p, ksum, kdiff)


def _longconv_kernel(z_ref, gate_ref, swz_ref, sbz_ref, swg_ref, sbg_ref, p_ref, pt_ref, ka_ref, kc_ref, ny_ref,
                     bias_ref, o_ref, pad_ref, zc_ref, zb_ref, y_ref, *, conv_z):
    s = pl.program_id(2)
    fb = ka_ref.shape[0]
    seq = z_ref.shape[0]
    nf = seq // fb

    @pl.when(s == 0)
    def _():
        if conv_z:
            _fill_padded(z_ref, pad_ref)
            w = swz_ref[...]
            b = sbz_ref[...]

            def body(i, carry):
                r0 = pl.multiple_of(i * CONV_ROWS, CONV_ROWS)
                y = _short_conv_rows(pad_ref, w, b, r0)
                zc_ref[pl.ds(r0, CONV_ROWS), :] = y
                zb_ref[pl.ds(r0, CONV_ROWS), :] = y.astype(BF16)
                return carry

            lax.fori_loop(0, seq // CONV_ROWS, body, 0)
        else:
            zb_ref[...] = z_ref[...].astype(BF16)

    @pl.when(s < nf)
    def _():
        spec = jnp.dot(p_ref[...], zb_ref[...], preferred_element_type=F32)
        xre = spec[:fb]
        xim = spec[fb:]
        ka = ka_ref[...]
        kc = kc_ref[...]
        is_dc = (lax.broadcasted_iota(jnp.int32, ka.shape, 0) + s * fb) == 0
        kd = jnp.where(is_dc, ny_ref[0:1, :], ka)
        base = pl.multiple_of(s * (2 * fb), 2 * fb)
        y_ref[pl.ds(base, fb), :] = (xre * ka - xim * kc).astype(BF16)
        y_ref[pl.ds(base + fb, fb), :] = (xre * kc + xim * kd).astype(BF16)

    @pl.when(s == nf)
    def _():
        _fill_padded(gate_ref, pad_ref)

    @pl.when(s >= nf)
    def _():
        r0 = pl.multiple_of((s - nf) * fb, fb)
        conv = jnp.dot(pt_ref[...], y_ref[...], preferred_element_type=F32)
        wg = swg_ref[...]
        bg = sbg_ref[...]
        bias = bias_ref[...]
        zsrc = zc_ref if conv_z else z_ref
        for k in range(fb // CONV_ROWS):
            rk = r0 + k * CONV_ROWS
            g = _short_conv_rows(pad_ref, wg, bg, rk)
            y = conv[k * CONV_ROWS:(k + 1) * CONV_ROWS, :]
            o_ref[pl.ds(rk, CONV_ROWS), :] = (g * (y + zsrc[pl.ds(rk, CONV_ROWS), :] * bias)).astype(o_ref.dtype)


def _long_conv(z, z_col, gate, gate_col, short_w, short_b, p, pt, ka, kc, ny, bias, order, conv_z,
               batch, seq_len, tc, out_dtype):
    c = bias.shape[1] // HY_ORDER
    t = z.shape[0]
    nc = c // tc
    nf = seq_len // FB
    zc_col = z_col if conv_z else 0

    def fwd(s):
        return jnp.minimum(s, nf - 1)

    def inv(s):
        return jnp.maximum(s - nf, 0)

    return pl.pallas_call(
        functools.partial(_longconv_kernel, conv_z=conv_z),
        grid=(batch, nc, 2 * nf),
        in_specs=[
            pl.BlockSpec((seq_len, tc), lambda b, j, s: (b, z_col * nc + j)),
            pl.BlockSpec((seq_len, tc), lambda b, j, s: (b, gate_col * nc + j)),
            pl.BlockSpec((HY_SHORT_W, tc), lambda b, j, s: (0, zc_col * nc + j)),
            pl.BlockSpec((1, tc), lambda b, j, s: (0, zc_col * nc + j)),
            pl.BlockSpec((HY_SHORT_W, tc), lambda b, j, s: (0, gate_col * nc + j)),
            pl.BlockSpec((1, tc), lambda b, j, s: (0, gate_col * nc + j)),
            pl.BlockSpec((2 * FB, seq_len), lambda b, j, s: (fwd(s), 0)),
            pl.BlockSpec((FB, 2 * seq_len), lambda b, j, s: (inv(s), 0)),
            pl.BlockSpec((FB, tc), lambda b, j, s: (fwd(s), order * nc + j)),
            pl.BlockSpec((FB, tc), lambda b, j, s: (fwd(s), order * nc + j)),
            pl.BlockSpec((SUBLANES, tc), lambda b, j, s: (0, order * nc + j)),
            pl.BlockSpec((1, tc), lambda b, j, s: (0, order * nc + j)),
        ],
        out_specs=pl.BlockSpec((seq_len, tc), lambda b, j, s: (b, j)),
        out_shape=jax.ShapeDtypeStruct((t, c), out_dtype),
        scratch_shapes=[
            pltpu.VMEM((seq_len + 2 * SUBLANES, tc), F32),
            pltpu.VMEM((seq_len, tc) if conv_z else (SUBLANES, LANES), F32),
            pltpu.VMEM((seq_len, tc), BF16),
            pltpu.VMEM((2 * seq_len, tc), BF16),
        ],
        compiler_params=_params(("parallel", "parallel", "arbitrary")),
        name="hyena_long_conv",
    )(z, gate, short_w, short_b, short_w, short_b, p, pt, ka, kc, ny, bias)


def _dft_matrix(seq_len):
    period = 2 * seq_len
    split = 64
    f = jnp.arange(seq_len, dtype=jnp.int32)[:, None]

    def trig(tvals):
        ang = ((f * tvals[None, :]) % period).astype(F32) * (2.0 * math.pi / period)
        return jnp.cos(ang), jnp.sin(ang)

    ca, sa = trig(jnp.arange(seq_len // split, dtype=jnp.int32) * split)
    cb, sb = trig(jnp.arange(split, dtype=jnp.int32))
    re = (ca[:, :, None] * cb[:, None, :] - sa[:, :, None] * sb[:, None, :]).reshape(seq_len, seq_len)
    im = -(sa[:, :, None] * cb[:, None, :] + ca[:, :, None] * sb[:, None, :]).reshape(seq_len, seq_len)
    nyq = jnp.where(jnp.arange(seq_len) % 2 == 0, 1.0, -1.0).astype(F32)
    im = im.at[0].set(nyq)
    nf = seq_len // FB
    p = jnp.stack([re.reshape(nf, FB, seq_len), im.reshape(nf, FB, seq_len)], axis=1)
    return p.reshape(2 * seq_len, seq_len).astype(BF16)


def _position_features(seq_len, width):
    pos = jnp.arange(seq_len, dtype=F32)
    t = pos / (seq_len - 1)
    w = 2.0 * math.pi * pos / seq_len
    f = jnp.linspace(1e-4, HY_BANDS - 1, HY_BANDS, dtype=F32)
    fw = w[:, None] * f[None, :]
    z = jnp.concatenate([t[:, None], jnp.cos(fw), -jnp.sin(fw)], axis=-1)
    return jnp.pad(z, ((0, 0), (0, width - z.shape[1])))


def _softplus(x):
    return jnp.maximum(x, 0.0) + jnp.log1p(jnp.exp(-jnp.abs(x)))


def _gelu_tanh(x):
    return 0.5 * x * (1.0 + jnp.tanh(math.sqrt(2.0 / math.pi) * (x + 0.044715 * (x * x * x))))


def _rglru_kernel(g_ref, x_ref, cw_ref, cb_ref, w_ref, bias_ref, lam_ref, o_ref,
                  xp_ref, xq_ref, a_ref, b_ref, hl_ref, al_ref, hs_ref):
    seq, width = x_ref.shape
    nck = SUBLANES
    chunk = seq // nck
    pitch = chunk + SCAN_PITCH_PAD
    slabs = width // LANES
    halo = RNN_CONV_W - 1

    for c in range(nck):
        for sl in range(slabs):
            xp_ref[sl, c * pitch:c * pitch + chunk, :] = x_ref[c * chunk:(c + 1) * chunk, sl * LANES:(sl + 1) * LANES]

    def to_chunk_major(i, carry):
        s0 = i * SCAN_GROUP
        row0 = pl.multiple_of((s0 + halo) * nck, nck)
        for k in range(SCAN_GROUP):
            for sl in range(slabs):
                xq_ref[pl.ds(row0 + k * nck, nck), sl * LANES:(sl + 1) * LANES] = (
                    xp_ref[sl, pl.ds(s0 + k, nck, stride=pitch), :])
        return carry

    lax.fori_loop(0, chunk // SCAN_GROUP, to_chunk_major, 0)

    sub = lax.broadcasted_iota(jnp.int32, (nck, width), 0)
    for k in range(halo):
        tail = xq_ref[(chunk + k) * nck:(chunk + k + 1) * nck, :]
        xq_ref[k * nck:(k + 1) * nck, :] = jnp.where(sub == 0, 0.0, pltpu.roll(tail, 1, 0))
        head = xq_ref[(halo + k) * nck:(halo + k + 1) * nck, :]
        xq_ref[(halo + chunk + k) * nck:(halo + chunk + k + 1) * nck, :] = jnp.where(
            sub == nck - 1, 0.0, pltpu.roll(head, nck - 1, 0))

    for d in range(2):
        ch = (0.5 * RNN_C) * _softplus(-lam_ref[d:d + 1, :])
        reset_row = 0 if d == 0 else seq - 1

        def gates(bi, carry):
            base = pl.multiple_of(bi * GATE_ROWS, GATE_ROWS)
            xc = cb_ref[d:d + 1, :]
            for j in range(RNN_CONV_W):
                step = j if d == 0 else 2 * halo - j
                xc = xc + xq_ref[pl.ds(base + step * nck, GATE_ROWS), :] * cw_ref[d, j:j + 1, :]
            gt = jnp.dot(xc.astype(BF16), w_ref[d, 0], preferred_element_type=F32) + bias_ref[d, 0]
            tr = jnp.tanh(gt[:, :width])
            ti = jnp.tanh(gt[:, width:])
            nla = tr * ch + ch
            a = jnp.exp2(nla * (-LOG2E))
            m2 = jnp.tanh(nla) * (a * a + 1.0)
            mult = jnp.where(m2 > 0.0, m2 * lax.rsqrt(m2), 0.0)
            ixc = (0.5 * ti + 0.5) * xc
            bt = mult * ixc
            a_ref[d, pl.ds(base, GATE_ROWS), :] = a
            b_ref[d, pl.ds(base, GATE_ROWS), :] = bt

            @pl.when(bi == reset_row // GATE_ROWS)
            def _():
                lo = (reset_row % GATE_ROWS) // nck * nck
                fixed = jnp.where(sub == reset_row % nck, ixc[lo:lo + nck, :], bt[lo:lo + nck, :])
                b_ref[d, pl.ds(base + lo, nck), :] = fixed

            return carry

        lax.fori_loop(0, seq // GATE_ROWS, gates, 0)

    def scan(i, carry):
        state = list(carry)
        group_rows = SCAN_GROUP * nck
        for k in range(SCAN_GROUP):
            for d in range(2):
                h, acc = state[2 * d], state[2 * d + 1]
                if d == 0:
                    row = pl.multiple_of(i * group_rows, nck) + k * nck
                else:
                    row = pl.multiple_of((chunk - SCAN_GROUP) * nck - i * group_rows, nck) + (SCAN_GROUP - 1 - k) * nck
                av = a_ref[d, pl.ds(row, nck), :]
                h = av * h + b_ref[d, pl.ds(row, nck), :]
                acc = acc * av
                hl_ref[d, pl.ds(row, nck), :] = h
                al_ref[d, pl.ds(row, nck), :] = acc
                state[2 * d], state[2 * d + 1] = h, acc
        return tuple(state)

    init = (jnp.zeros((nck, width), F32), jnp.ones((nck, width), F32)) * 2
    ends = lax.fori_loop(0, chunk // SCAN_GROUP, scan, init)

    carries = []
    for d in range(2):
        h_end, a_end = ends[2 * d], ends[2 * d + 1]
        rows = [None] * nck
        cur = jnp.zeros((1, width), F32)
        for c in (range(nck) if d == 0 else range(nck - 1, -1, -1)):
            rows[c] = cur
            cur = a_end[c:c + 1, :] * cur + h_end[c:c + 1, :]
        carries.append(jnp.concatenate(rows, axis=0))

    def fold(i, carry):
        s0 = i * SCAN_GROUP
        row0 = pl.multiple_of(s0 * nck, nck)
        for k in range(SCAN_GROUP):
            rows = pl.ds(row0 + k * nck, nck)
            hv = (hl_ref[0, rows, :] + al_ref[0, rows, :] * carries[0]
                  + hl_ref[1, rows, :] + al_ref[1, rows, :] * carries[1])
            for sl in range(slabs):
                hs_ref[sl, pl.ds(s0 + k, nck, stride=pitch), :] = hv[:, sl * LANES:(sl + 1) * LANES]
        return carry

    lax.fori_loop(0, chunk // SCAN_GROUP, fold, 0)

    def finish(c, carry):
        r0 = pl.multiple_of(c * chunk, chunk)
        p0 = pl.multiple_of(c * pitch, SUBLANES)
        gate = _gelu_tanh(g_ref[pl.ds(r0, chunk), :])
        for sl in range(slabs):
            o_ref[pl.ds(r0, chunk), sl * LANES:(sl + 1) * LANES] = (
                hs_ref[sl, pl.ds(p0, chunk), :] * gate[:, sl * LANES:(sl + 1) * LANES]).astype(o_ref.dtype)
        return carry

    lax.fori_loop(0, nck, finish, 0)


def _rglru(u, conv_w, conv_b, w_gates, b_gates, lam, batch, seq_len):
    t, two_d = u.shape
    d = two_d // 2
    width = d // RNN_HEADS
    slabs = width // LANES
    chunk = seq_len // SUBLANES
    pitched = SUBLANES * (chunk + SCAN_PITCH_PAD)
    halo_rows = 2 * (RNN_CONV_W - 1) * SUBLANES
    return pl.pallas_call(
        _rglru_kernel,
        grid=(batch, RNN_HEADS),
        in_specs=[
            pl.BlockSpec((seq_len, width), lambda b, h: (b, h)),
            pl.BlockSpec((seq_len, width), lambda b, h: (b, RNN_HEADS + h)),
            pl.BlockSpec((2, RNN_CONV_W, width), lambda b, h: (0, 0, h)),
            pl.BlockSpec((2, width), lambda b, h: (0, h)),
            pl.BlockSpec((2, 1, width, 2 * width), lambda b, h: (0, h, 0, 0)),
            pl.BlockSpec((2, 1, 1, 2 * width), lambda b, h: (0, h, 0, 0)),
            pl.BlockSpec((2, width), lambda b, h: (0, h)),
        ],
        out_specs=pl.BlockSpec((seq_len, width), lambda b, h: (b, h)),
        out_shape=jax.ShapeDtypeStruct((t, d), BF16),
        scratch_shapes=[
            pltpu.VMEM((slabs, pitched, LANES), F32),
            pltpu.VMEM((seq_len + halo_rows, width), F32),
            pltpu.VMEM((2, seq_len, width), F32),
            pltpu.VMEM((2, seq_len, width), F32),
            pltpu.VMEM((2, seq_len, width), F32),
            pltpu.VMEM((2, seq_len, width), F32),
            pltpu.VMEM((slabs, pitched, LANES), F32),
        ],
        compiler_params=_params(("parallel", "parallel")),
        name="rglru",
    )(u, u, conv_w, conv_b, w_gates, b_gates, lam)


def _deinterleave_heads(w, n_heads):
    lead = w.shape[:-1]
    w = w.reshape(lead + (n_heads, HEAD_DIM // 2, 2))
    w = jnp.swapaxes(w, -1, -2)
    return w.reshape(lead + (n_heads * HEAD_DIM,))


def _rope_tables(seq_len):
    rows = seq_len // GRID_W
    row = jnp.repeat(jnp.arange(rows, dtype=F32), GRID_W)
    col = jnp.tile(jnp.arange(GRID_W, dtype=F32), rows)
    axis_dim = HEAD_DIM // 2
    omega = ROPE_THETA ** (-jnp.arange(0, axis_dim, 2, dtype=F32) / axis_dim)
    ang = jnp.concatenate([row[:, None] * omega[None], col[:, None] * omega[None]], axis=-1)
    c, s = jnp.cos(ang), jnp.sin(ang)
    return jnp.concatenate([c, c], axis=-1), jnp.concatenate([-s, s], axis=-1)


def _even_mixer(x, batch, seq_len, norm_g, w_in, short_w, short_b, f_w1, f_b1, f_w2, f_b2, f_w3, f_freq,
                hy_bias, q_g, k_g, w_out):
    d = x.shape[1]
    hy_d = hy_bias.shape[1]
    s0 = (HY_ORDER + 1) * hy_d
    nq = N_Q_HEADS * HEAD_DIM
    nk = N_KV_HEADS * HEAD_DIM
    g = norm_g.reshape(1, d)

    u_hy = _norm_proj(x, g, w_in[:, :s0].astype(BF16), F32, "hyena_in_proj")

    hid = LANES
    hpad = hid - f_w1.shape[1]
    zfeat = _position_features(seq_len, LANES)
    w1 = jnp.pad(f_w1, ((0, LANES - f_w1.shape[0]), (0, hpad)))
    b1 = jnp.pad(f_b1, (0, hpad)).reshape(1, hid)
    w2 = jnp.pad(f_w2, ((0, 0), (0, hpad), (0, hpad)))
    b2 = jnp.pad(f_b2, ((0, 0), (0, hpad))).reshape(HY_N_INNER, 1, hid)
    w3 = jnp.pad(f_w3, ((0, hpad), (0, 0)))
    freq = jnp.pad(f_freq, (0, hpad)).reshape(1, hid)
    max_decay = math.log(HY_DECAY_TARGET) / HY_FAST_DECAY
    min_decay = math.log(HY_DECAY_TARGET) / HY_SLOW_DECAY
    deltas = jnp.abs(jnp.linspace(min_decay, max_decay, hy_d, dtype=F32))
    deltas2 = jnp.tile(deltas, HY_ORDER).reshape(1, HY_ORDER * hy_d)
    ksum, kdiff = _hyena_filter_taps(zfeat, w1, b1, w2, b2, w3, freq, deltas2, seq_len, 256)
    p = _dft_matrix(seq_len)
    ka, kc, ny = _hyena_spectrum(p, ksum, kdiff, 1024)
    bias = hy_bias.reshape(1, HY_ORDER * hy_d)
    sb = short_b.reshape(1, s0)
    pt = p.T
    z1 = _long_conv(u_hy, 0, u_hy, 1, short_w, sb, p, pt, ka, kc, ny, bias, 0, True, batch, seq_len, 512, F32)
    y_hy = _long_conv(z1, 0, u_hy, 2, short_w, sb, p, pt, ka, kc, ny, bias, 1, False, batch, seq_len, 512, BF16)

    w_qkv = jnp.concatenate([
        _deinterleave_heads(w_in[:, s0:s0 + nq], N_Q_HEADS),
        _deinterleave_heads(w_in[:, s0 + nq:s0 + nq + nk], N_KV_HEADS),
        w_in[:, s0 + nq + nk:]], axis=1).astype(BF16)
    qg = _deinterleave_heads(q_g, 1) * (HEAD_DIM ** -0.5 * LOG2E)
    kg = _deinterleave_heads(k_g, 1)
    head_gain = jnp.concatenate([jnp.tile(qg, N_Q_HEADS), jnp.tile(kg, N_KV_HEADS),
                                 jnp.ones((nk,), F32)]).reshape(1, nq + 2 * nk)
    cos, sin = _rope_tables(seq_len)
    qkv = _qkv_proj(x, g, w_qkv, head_gain, cos, sin, seq_len)
    y_at = _attention(qkv, batch, seq_len, 512)

    return _out_proj(x, [y_hy, y_at], w_out.astype(BF16), "even_out_proj")


def _odd_mixer(x, batch, seq_len, norm_g, w_in, conv_w, conv_b, wa, ba, wx, bx, lam, w_out):
    d = x.shape[1]
    u = _norm_proj(x, norm_g.reshape(1, d), w_in.astype(BF16), F32, "rglru_in_proj")
    w_gates = (0.5 * jnp.concatenate([wa, wx], axis=-1)).astype(BF16)
    b_gates = 0.5 * jnp.concatenate([ba, bx], axis=-1)[:, :, None, :]
    y = _rglru(u, conv_w, conv_b, w_gates, b_gates, lam, batch, seq_len)
    return _out_proj(x, [y], w_out.astype(BF16), "rglru_out_proj")


def kernel(x_prompt, x_sample, ffn_norm, ffn_w_gate, ffn_w_up, ffn_w_down, even_norm, even_w_in, hy_short_w, hy_short_b, hy_filt_w1, hy_filt_b1, hy_filt_w2, hy_filt_b2, hy_filt_w3, hy_filt_freq, hy_bias, q_norm, k_norm, even_w_out, odd_norm, odd_w_in, rg_conv_w, rg_conv_b, rg_wa, rg_ba, rg_wx, rg_bx, rg_lambda, odd_w_out):
    bp, seq_len, d = x_prompt.shape
    bs = x_sample.shape[0]
    assert x_sample.shape[1:] == (seq_len, d)
    batch = bp + bs
    rows_p = bp * seq_len

    depth = ffn_norm.shape[0]
    ff = ffn_w_gate.shape[-1]
    fp = -(-ff // TF) * TF
    wg = _cast_pad(ffn_w_gate.reshape(depth * 2, d, ff), d, fp, 1024, TF)
    wu = _cast_pad(ffn_w_up.reshape(depth * 2, d, ff), d, fp, 1024, TF)
    wd = _cast_pad(ffn_w_down.reshape(depth * 2, ff, d), fp, d, TF, d)
    fg = ffn_norm.reshape(depth * 2, 1, d)
    rows_s = bs * seq_len
    total = rows_p + rows_s

    x = None
    for layer in range(depth):
        if layer == 0:
            x = _ffn_half(x_prompt.reshape(rows_p, d), fg, 0, wg, wu, wd, out_rows=total)
            x = _ffn_half(x_sample.reshape(rows_s, d), fg, 0, wg, wu, wd, out_rows=total, out_row0=rows_p, dest=x)
        else:
            x = _ffn_half(x, fg, 2 * layer, wg, wu, wd)
        j = layer // 2
        if layer % 2 == 0:
            x = _even_mixer(x, batch, seq_len, even_norm[j], even_w_in[j], hy_short_w[j], hy_short_b[j],
                            hy_filt_w1[j], hy_filt_b1[j], hy_filt_w2[j], hy_filt_b2[j], hy_filt_w3[j],
                            hy_filt_freq[j], hy_bias[j], q_norm[j], k_norm[j], even_w_out[j])
        else:
            x = _odd_mixer(x, batch, seq_len, odd_norm[j], odd_w_in[j], rg_conv_w[j], rg_conv_b[j],
                           rg_wa[j], rg_ba[j], rg_wx[j], rg_bx[j], rg_lambda[j], odd_w_out[j])
        if layer == depth - 1:
            y_prompt = _ffn_half(x, fg, 2 * layer + 1, wg, wu, wd, rows=rows_p)
            y_sample = _ffn_half(x, fg, 2 * layer + 1, wg, wu, wd, rows=rows_s, x_row0=rows_p)
        else:
            x = _ffn_half(x, fg, 2 * layer + 1, wg, wu, wd)

    return (y_prompt.reshape(bp, seq_len, d), y_sample.reshape(bs, seq_len, d))
```

```python
import functools
import math

import jax
import jax.numpy as jnp
from jax import lax
from jax.experimental import pallas as pl
from jax.experimental.pallas import tpu as pltpu

F32 = jnp.float32
BF16 = jnp.bfloat16

NORM_EPS = 1e-6
GRID_W = 64
HEAD_DIM = 128
N_Q_HEADS = 8
N_KV_HEADS = 2
GQA_GROUP = N_Q_HEADS // N_KV_HEADS
ROPE_THETA = 10000.0
HY_ORDER = 2
HY_SHORT_W = 3
HY_EMB_DIM = 33
HY_BANDS = (HY_EMB_DIM - 1) // 2
HY_N_INNER = 2
HY_FAST_DECAY = 0.3
HY_SLOW_DECAY = 1.5
HY_DECAY_TARGET = 1e-2
RNN_HEADS = 8
RNN_CONV_W = 4
RNN_C = 8.0
LOG2E = 1.4426950408889634

LANES = 128
SUBLANES = 8
VMEM_LIMIT = 56 * 1024 * 1024

TM = 512
TM_FFN = 1024
TF = 512
TN = 1024
ATT_KC = 256
FB = 256
CONV_ROWS = 64
GATE_ROWS = 512
SCAN_GROUP = 8
SCAN_PITCH_PAD = 8


def _params(sem, vmem=VMEM_LIMIT):
    return pltpu.CompilerParams(dimension_semantics=sem, vmem_limit_bytes=vmem)


def _resident(shape):
    return pl.BlockSpec(shape, lambda *_: (0,) * len(shape), pipeline_mode=pl.Buffered(1))


def _rms(x, g):
    ms = jnp.mean(x * x, axis=-1, keepdims=True)
    return x * lax.rsqrt(ms + NORM_EPS) * g


def _cast_pad_kernel(w_ref, o_ref, *, rows, cols):
    br, bc = o_ref.shape
    r = lax.broadcasted_iota(jnp.int32, (br, bc), 0) + pl.program_id(1) * br
    c = lax.broadcasted_iota(jnp.int32, (br, bc), 1) + pl.program_id(2) * bc
    w = jnp.where(r < rows, jnp.where(c < cols, w_ref[...], 0.0), 0.0)
    o_ref[...] = w.astype(o_ref.dtype)


def _cast_pad(w, rows_out, cols_out, br, bc):
    n, rows, cols = w.shape
    return pl.pallas_call(
        functools.partial(_cast_pad_kernel, rows=rows, cols=cols),
        grid=(n, rows_out // br, cols_out // bc),
        in_specs=[pl.BlockSpec((None, br, bc), lambda k, i, j: (k, i, j))],
        out_specs=pl.BlockSpec((None, br, bc), lambda k, i, j: (k, i, j)),
        out_shape=jax.ShapeDtypeStruct((n, rows_out, cols_out), BF16),
        compiler_params=_params(("parallel", "parallel", "parallel")),
        name="cast_pad",
    )(w)


def _ffn_kernel(x_ref, g_ref, wg_ref, wu_ref, wd_ref, *rest):
    o_ref, n_ref = rest[-2:]

    @pl.when(pl.program_id(1) == 0)
    def _():
        x = x_ref[...]
        n_ref[...] = _rms(x, g_ref[...]).astype(BF16)
        o_ref[...] = x

    n = n_ref[...]
    hg = jnp.dot(n, wg_ref[...], preferred_element_type=F32)
    hu = jnp.dot(n, wu_ref[...], preferred_element_type=F32)
    h = ((hg * jax.nn.sigmoid(hg)) * (hu * 0.5)).astype(BF16)
    o_ref[...] += jnp.dot(h, wd_ref[...], preferred_element_type=F32)


def _ffn_half(x, g, w_idx, wg, wu, wd, *, rows=None, x_row0=0, out_rows=None, out_row0=0, dest=None):
    d = x.shape[1]
    fp = wg.shape[2]
    rows = x.shape[0] if rows is None else rows
    out_rows = rows if out_rows is None else out_rows
    xb = x_row0 // TM_FFN
    ob = out_row0 // TM_FFN
    in_specs = [
        pl.BlockSpec((TM_FFN, d), lambda i, j: (i + xb, 0)),
        pl.BlockSpec((None, 1, d), lambda i, j: (w_idx, 0, 0)),
        pl.BlockSpec((None, d, TF), lambda i, j: (w_idx, 0, j)),
        pl.BlockSpec((None, d, TF), lambda i, j: (w_idx, 0, j)),
        pl.BlockSpec((None, TF, d), lambda i, j: (w_idx, j, 0)),
    ]
    args = [x, g, wg, wu, wd]
    aliases = {}
    if dest is not None:
        in_specs.append(pl.BlockSpec(memory_space=pl.ANY))
        args.append(dest)
        aliases = {len(args) - 1: 0}
    return pl.pallas_call(
        _ffn_kernel,
        grid=(rows // TM_FFN, fp // TF),
        in_specs=in_specs,
        out_specs=pl.BlockSpec((TM_FFN, d), lambda i, j: (i + ob, 0)),
        out_shape=jax.ShapeDtypeStruct((out_rows, d), F32),
        scratch_shapes=[pltpu.VMEM((TM_FFN, d), BF16)],
        input_output_aliases=aliases,
        compiler_params=_params(("parallel", "arbitrary")),
        name="ffn_half",
    )(*args)


def _proj_kernel(x_ref, g_ref, w_ref, o_ref):
    n = _rms(x_ref[...], g_ref[...]).astype(BF16)
    for c in range(0, w_ref.shape[1], TN):
        o_ref[:, c:c + TN] = jnp.dot(n, w_ref[:, c:c + TN], preferred_element_type=F32).astype(o_ref.dtype)


def _norm_proj(x, g, w, out_dtype, name):
    t, d = x.shape
    n = w.shape[1]
    return pl.pallas_call(
        _proj_kernel,
        grid=(t // TM,),
        in_specs=[
            pl.BlockSpec((TM, d), lambda i: (i, 0)),
            _resident((1, d)),
            _resident((d, n)),
        ],
        out_specs=pl.BlockSpec((TM, n), lambda i: (i, 0)),
        out_shape=jax.ShapeDtypeStruct((t, n), out_dtype),
        compiler_params=_params(("parallel",)),
        name=name,
    )(x, g, w)


def _qkv_kernel(x_ref, g_ref, w_ref, hg_ref, cos_ref, sin_ref, o_ref, *, n_rope):
    n = _rms(x_ref[...], g_ref[...]).astype(BF16)
    u = jnp.dot(n, w_ref[...], preferred_element_type=F32)
    c = cos_ref[...]
    s = sin_ref[...]
    for h in range(u.shape[1] // HEAD_DIM):
        sl = slice(h * HEAD_DIM, (h + 1) * HEAD_DIM)
        uh = u[:, sl]
        if h < n_rope:
            ms = jnp.mean(uh * uh, axis=-1, keepdims=True)
            uh = uh * lax.rsqrt(ms + NORM_EPS) * hg_ref[:, sl]
            uh = uh * c + pltpu.roll(uh, HEAD_DIM // 2, 1) * s
        o_ref[:, sl] = uh.astype(o_ref.dtype)


def _qkv_proj(x, g, w, head_gain, cos, sin, seq_len):
    t, d = x.shape
    n = w.shape[1]
    tiles_per_seq = seq_len // TM
    n_rope = N_Q_HEADS + N_KV_HEADS
    return pl.pallas_call(
        functools.partial(_qkv_kernel, n_rope=n_rope),
        grid=(t // TM,),
        in_specs=[
            pl.BlockSpec((TM, d), lambda i: (i, 0)),
            _resident((1, d)),
            _resident((d, n)),
            _resident((1, n)),
            pl.BlockSpec((TM, HEAD_DIM), lambda i: (i % tiles_per_seq, 0)),
            pl.BlockSpec((TM, HEAD_DIM), lambda i: (i % tiles_per_seq, 0)),
        ],
        out_specs=pl.BlockSpec((TM, n), lambda i: (i, 0)),
        out_shape=jax.ShapeDtypeStruct((t, n), BF16),
        compiler_params=_params(("parallel",)),
        name="qkv_proj",
    )(x, g, w, head_gain, cos, sin)


def _outproj_kernel(*refs, n_y):
    r_ref = refs[0]
    y_refs = refs[1:1 + n_y]
    w_ref = refs[1 + n_y]
    o_ref = refs[2 + n_y]
    ys = [y_ref[...] for y_ref in y_refs]
    for c in range(0, o_ref.shape[1], TN):
        acc = r_ref[:, c:c + TN]
        k0 = 0
        for y in ys:
            k = y.shape[1]
            acc = acc + jnp.dot(y, w_ref[k0:k0 + k, c:c + TN], preferred_element_type=F32)
            k0 += k
        o_ref[:, c:c + TN] = acc


def _out_proj(res, ys, w, name):
    t, d = res.shape
    return pl.pallas_call(
        functools.partial(_outproj_kernel, n_y=len(ys)),
        grid=(t // TM,),
        in_specs=[pl.BlockSpec((TM, d), lambda i: (i, 0))]
        + [pl.BlockSpec((TM, y.shape[1]), lambda i: (i, 0)) for y in ys]
        + [_resident(w.shape)],
        out_specs=pl.BlockSpec((TM, d), lambda i: (i, 0)),
        out_shape=jax.ShapeDtypeStruct((t, d), F32),
        compiler_params=_params(("parallel",)),
        name=name,
    )(res, *ys, w)


def _attn_kernel(q_ref, k_ref, v_ref, o_ref, st_ref):
    seq = k_ref.shape[0]
    n_heads = q_ref.shape[1] // HEAD_DIM
    nt = (((1,), (1,)), ((), ()))
    tn = (((0,), (0,)), ((), ()))
    chunks = [slice(c * ATT_KC, (c + 1) * ATT_KC) for c in range(seq // ATT_KC)]

    def scores(g, rows, slot, m):
        s = lax.dot_general(k_ref[rows, :], q_ref[:, g * HEAD_DIM:(g + 1) * HEAD_DIM], nt,
                            preferred_element_type=F32)
        st_ref[slot, rows, :] = s
        cm = jnp.max(s, axis=0, keepdims=True)
        return cm if m is None else jnp.maximum(m, cm)

    m = None
    for rows in chunks:
        m = scores(0, rows, 0, m)
    for g in range(n_heads):
        cur = g % 2
        acc = l = m_next = None
        for rows in chunks:
            if g + 1 < n_heads:
                m_next = scores(g + 1, rows, 1 - cur, m_next)
            p = jnp.exp2(st_ref[cur, rows, :] - m)
            ps = jnp.sum(p, axis=0, keepdims=True)
            pv = lax.dot_general(v_ref[rows, :], p.astype(BF16), tn, preferred_element_type=F32)
            l = ps if l is None else l + ps
            acc = pv if acc is None else acc + pv
        o_ref[:, g * HEAD_DIM:(g + 1) * HEAD_DIM] = (acc / l).T.astype(o_ref.dtype)
        m = m_next


def _attention(qkv, batch, seq_len, tq):
    t = qkv.shape[0]
    gw = GQA_GROUP * HEAD_DIM
    nq = seq_len // tq
    k_off = N_Q_HEADS
    v_off = N_Q_HEADS + N_KV_HEADS
    return pl.pallas_call(
        _attn_kernel,
        grid=(batch, N_KV_HEADS, nq),
        in_specs=[
            pl.BlockSpec((tq, gw), lambda b, h, i: (b * nq + i, h)),
            pl.BlockSpec((seq_len, HEAD_DIM), lambda b, h, i: (b, k_off + h)),
            pl.BlockSpec((seq_len, HEAD_DIM), lambda b, h, i: (b, v_off + h)),
        ],
        out_specs=pl.BlockSpec((tq, gw), lambda b, h, i: (b * nq + i, h)),
        out_shape=jax.ShapeDtypeStruct((t, N_Q_HEADS * HEAD_DIM), BF16),
        scratch_shapes=[pltpu.VMEM((2, seq_len, tq), F32)],
        compiler_params=_params(("parallel", "parallel", "arbitrary")),
        name="attention",
    )(qkv, qkv, qkv)


def _fill_padded(src_ref, pad_ref):
    seq = src_ref.shape[0]
    zeros = jnp.zeros((SUBLANES, pad_ref.shape[1]), F32)
    pad_ref[0:SUBLANES, :] = zeros
    pad_ref[seq + SUBLANES:seq + 2 * SUBLANES, :] = zeros
    pad_ref[SUBLANES:seq + SUBLANES, :] = src_ref[...]


def _short_conv_rows(pad_ref, w, b, r0):
    n = CONV_ROWS + 2 * SUBLANES
    win = pad_ref[pl.ds(r0, n), :]
    taps = (pltpu.roll(win, 1, 0), win, pltpu.roll(win, n - 1, 0))
    y = b
    for j in range(HY_SHORT_W):
        y = y + taps[j][SUBLANES:SUBLANES + CONV_ROWS, :] * w[j:j + 1, :]
    return y


def _filter_kernel(z_ref, w1_ref, b1_ref, w2_ref, b2_ref, w3_ref, fr_ref, dl_ref, ks_ref, kd_ref, *, seq_len):
    hi = lax.Precision.HIGHEST
    fr = fr_ref[...]
    h = jnp.sin(fr * (jnp.dot(z_ref[...], w1_ref[...], precision=hi, preferred_element_type=F32) + b1_ref[...]))
    for i in range(HY_N_INNER):
        h = jnp.sin(fr * (jnp.dot(h, w2_ref[i], precision=hi, preferred_element_type=F32) + b2_ref[i]))
    k = jnp.dot(h, w3_ref[...], precision=hi, preferred_element_type=F32)
    tl, half = ks_ref.shape
    pos = lax.broadcasted_iota(jnp.int32, (tl, half), 0) + pl.program_id(0) * tl
    tnorm = pos.astype(F32) / (seq_len - 1)
    decay = jnp.exp(-tnorm * dl_ref[...])
    kf = k[:, :half] * decay
    kb = jnp.where(pos == 0, 0.0, k[:, half:] * decay)
    ks_ref[...] = (kf + kb).astype(ks_ref.dtype)
    kd_ref[...] = (kf - kb).astype(kd_ref.dtype)


def _hyena_filter_taps(zfeat, w1, b1, w2, b2, w3, freq, deltas2, seq_len, tl):
    half = deltas2.shape[1]
    return pl.pallas_call(
        functools.partial(_filter_kernel, seq_len=seq_len),
        grid=(seq_len // tl,),
        in_specs=[
            pl.BlockSpec((tl, zfeat.shape[1]), lambda i: (i, 0)),
            _resident(w1.shape),
            _resident(b1.shape),
            _resident(w2.shape),
            _resident(b2.shape),
            _resident(w3.shape),
            _resident(freq.shape),
            _resident(deltas2.shape),
        ],
        out_specs=[pl.BlockSpec((tl, half), lambda i: (i, 0))] * 2,
        out_shape=[jax.ShapeDtypeStruct((seq_len, half), BF16)] * 2,
        compiler_params=_params(("parallel",)),
        name="hyena_filter_taps",
    )(zfeat, w1, b1, w2, b2, w3, freq, deltas2)


def _spectrum_kernel(p_ref, ks_ref, kd_ref, ka_ref, kc_ref, ny_ref, *, period):
    fb = ka_ref.shape[0]
    ks = ks_ref[...]
    re = jnp.dot(p_ref[0:fb, :], ks, preferred_element_type=F32)
    im = jnp.dot(p_ref[fb:2 * fb, :], kd_ref[...], preferred_element_type=F32)
    ny = jnp.dot(p_ref[fb:fb + SUBLANES, :], ks, preferred_element_type=F32)
    is_dc = (lax.broadcasted_iota(jnp.int32, re.shape, 0) + pl.program_id(1) * fb) == 0
    ka_ref[...] = re * jnp.where(is_dc, 1.0 / period, 2.0 / period)
    kc_ref[...] = jnp.where(is_dc, 0.0, im * (2.0 / period))
    ny_ref[...] = ny * (1.0 / period)


def _hyena_spectrum(p, ksum, kdiff, tcs):
    seq_len, cols = ksum.shape
    nf = p.shape[0] // (2 * FB)
    return pl.pallas_call(
        functools.partial(_spectrum_kernel, period=2 * seq_len),
        grid=(cols // tcs, nf),
        in_specs=[
            pl.BlockSpec((2 * FB, seq_len), lambda c, f: (f, 0)),
            pl.BlockSpec((seq_len, tcs), lambda c, f: (0, c)),
            pl.BlockSpec((seq_len, tcs), lambda c, f: (0, c)),
        ],
        out_specs=[
            pl.BlockSpec((FB, tcs), lambda c, f: (f, c)),
            pl.BlockSpec((FB, tcs), lambda c, f: (f, c)),
            pl.BlockSpec((SUBLANES, tcs), lambda c, f: (f, c)),
        ],
        out_shape=[
            jax.ShapeDtypeStruct((nf * FB, cols), F32),
            jax.ShapeDtypeStruct((nf * FB, cols), F32),
            jax.ShapeDtypeStruct((nf * SUBLANES, cols), F32),
        ],
        compiler_params=_params(("parallel", "parallel")),
        name="hyena_filter_spectrum",
    )(p, ksum, kdiff)


def _longconv_kernel(z_ref, gate_ref, swz_ref, sbz_ref, swg_ref, sbg_ref, pt_ref, ka_ref, kc_ref, ny_ref,
                     bias_ref, o_ref, pad_ref, zc_ref, zb_ref, y_ref, *, conv_z):
    seq = z_ref.shape[0]
    nf = seq // FB

    if conv_z:
        _fill_padded(z_ref, pad_ref)
        w = swz_ref[...]
        b = sbz_ref[...]

        def body(i, carry):
            r0 = pl.multiple_of(i * CONV_ROWS, CONV_ROWS)
            y = _short_conv_rows(pad_ref, w, b, r0)
            zc_ref[pl.ds(r0, CONV_ROWS), :] = y
            zb_ref[pl.ds(r0, CONV_ROWS), :] = y.astype(BF16)
            return carry

        lax.fori_loop(0, seq // CONV_ROWS, body, 0)
    else:
        zb_ref[...] = z_ref[...].astype(BF16)
    _fill_padded(gate_ref, pad_ref)

    tn = (((0,), (0,)), ((), ()))
    for f in range(nf):
        cols = slice(f * 2 * FB, (f + 1) * 2 * FB)
        spec = lax.dot_general(pt_ref[:, cols], zb_ref[...], tn, preferred_element_type=F32)
        xre = spec[:FB]
        xim = spec[FB:]
        ka = ka_ref[f * FB:(f + 1) * FB, :]
        kc = kc_ref[f * FB:(f + 1) * FB, :]
        kd = ka
        if f == 0:
            is_dc = lax.broadcasted_iota(jnp.int32, ka.shape, 0) == 0
            kd = jnp.where(is_dc, ny_ref[0:1, :], ka)
        y_ref[f * 2 * FB:f * 2 * FB + FB, :] = (xre * ka - xim * kc).astype(BF16)
        y_ref[f * 2 * FB + FB:(f + 1) * 2 * FB, :] = (xre * kc + xim * kd).astype(BF16)

    wg = swg_ref[...]
    bg = sbg_ref[...]
    bias = bias_ref[...]
    zsrc = zc_ref if conv_z else z_ref
    for r in range(nf):
        conv = jnp.dot(pt_ref[r * FB:(r + 1) * FB, :], y_ref[...], preferred_element_type=F32)
        for k in range(FB // CONV_ROWS):
            r0 = r * FB + k * CONV_ROWS
            g = _short_conv_rows(pad_ref, wg, bg, r0)
            y = conv[k * CONV_ROWS:(k + 1) * CONV_ROWS, :]
            o_ref[r0:r0 + CONV_ROWS, :] = (g * (y + zsrc[r0:r0 + CONV_ROWS, :] * bias)).astype(o_ref.dtype)


def _long_conv(z, z_col, gate, gate_col, short_w, short_b, pt, ka, kc, ny, bias, order, conv_z,
               batch, seq_len, tc, out_dtype):
    c = bias.shape[1] // HY_ORDER
    t = z.shape[0]
    nc = c // tc
    zc_col = z_col if conv_z else 0
    return pl.pallas_call(
        functools.partial(_longconv_kernel, conv_z=conv_z),
        grid=(nc, batch),
        in_specs=[
            pl.BlockSpec((seq_len, tc), lambda j, b: (b, z_col * nc + j)),
            pl.BlockSpec((seq_len, tc), lambda j, b: (b, gate_col * nc + j)),
            pl.BlockSpec((HY_SHORT_W, tc), lambda j, b: (0, zc_col * nc + j)),
            pl.BlockSpec((1, tc), lambda j, b: (0, zc_col * nc + j)),
            pl.BlockSpec((HY_SHORT_W, tc), lambda j, b: (0, gate_col * nc + j)),
            pl.BlockSpec((1, tc), lambda j, b: (0, gate_col * nc + j)),
            _resident(pt.shape),
            pl.BlockSpec((seq_len, tc), lambda j, b: (0, order * nc + j)),
            pl.BlockSpec((seq_len, tc), lambda j, b: (0, order * nc + j)),
            pl.BlockSpec((SUBLANES, tc), lambda j, b: (0, order * nc + j)),
            pl.BlockSpec((1, tc), lambda j, b: (0, order * nc + j)),
        ],
        out_specs=pl.BlockSpec((seq_len, tc), lambda j, b: (b, j)),
        out_shape=jax.ShapeDtypeStruct((t, c), out_dtype),
        scratch_shapes=[
            pltpu.VMEM((seq_len + 2 * SUBLANES, tc), F32),
            pltpu.VMEM((seq_len, tc) if conv_z else (SUBLANES, LANES), F32),
            pltpu.VMEM((seq_len, tc), BF16),
            pltpu.VMEM((2 * seq_len, tc), BF16),
        ],
        compiler_params=_params(("parallel", "parallel")),
        name="hyena_long_conv",
    )(z, gate, short_w, short_b, short_w, short_b, pt, ka, kc, ny, bias)


def _dft_matrix(seq_len):
    period = 2 * seq_len
    split = 64
    f = jnp.arange(seq_len, dtype=jnp.int32)[:, None]

    def trig(tvals):
        ang = ((f * tvals[None, :]) % period).astype(F32) * (2.0 * math.pi / period)
        return jnp.cos(ang), jnp.sin(ang)

    ca, sa = trig(jnp.arange(seq_len // split, dtype=jnp.int32) * split)
    cb, sb = trig(jnp.arange(split, dtype=jnp.int32))
    re = (ca[:, :, None] * cb[:, None, :] - sa[:, :, None] * sb[:, None, :]).reshape(seq_len, seq_len)
    im = -(sa[:, :, None] * cb[:, None, :] + ca[:, :, None] * sb[:, None, :]).reshape(seq_len, seq_len)
    nyq = jnp.where(jnp.arange(seq_len) % 2 == 0, 1.0, -1.0).astype(F32)
    im = im.at[0].set(nyq)
    nf = seq_len // FB
    p = jnp.stack([re.reshape(nf, FB, seq_len), im.reshape(nf, FB, seq_len)], axis=1)
    return p.reshape(2 * seq_len, seq_len).astype(BF16)


def _position_features(seq_len, width):
    pos = jnp.arange(seq_len, dtype=F32)
    t = pos / (seq_len - 1)
    w = 2.0 * math.pi * pos / seq_len
    f = jnp.linspace(1e-4, HY_BANDS - 1, HY_BANDS, dtype=F32)
    fw = w[:, None] * f[None, :]
    z = jnp.concatenate([t[:, None], jnp.cos(fw), -jnp.sin(fw)], axis=-1)
    return jnp.pad(z, ((0, 0), (0, width - z.shape[1])))


def _softplus(x):
    return jnp.maximum(x, 0.0) + jnp.log1p(jnp.exp(-jnp.abs(x)))


def _gelu_tanh(x):
    return 0.5 * x * (1.0 + jnp.tanh(math.sqrt(2.0 / math.pi) * (x + 0.044715 * (x * x * x))))


def _rglru_kernel(g_ref, x_ref, cw_ref, cb_ref, w_ref, bias_ref, lam_ref, o_ref,
                  xp_ref, xq_ref, a_ref, b_ref, hl_ref, al_ref, hs_ref):
    seq, width = x_ref.shape
    nck = SUBLANES
    chunk = seq // nck
    pitch = chunk + SCAN_PITCH_PAD
    slabs = width // LANES
    halo = RNN_CONV_W - 1

    for c in range(nck):
        for sl in range(slabs):
            xp_ref[sl, c * pitch:c * pitch + chunk, :] = x_ref[c * chunk:(c + 1) * chunk, sl * LANES:(sl + 1) * LANES]

    def to_chunk_major(i, carry):
        s0 = i * SCAN_GROUP
        row0 = pl.multiple_of((s0 + halo) * nck, nck)
        for k in range(SCAN_GROUP):
            for sl in range(slabs):
                xq_ref[pl.ds(row0 + k * nck, nck), sl * LANES:(sl + 1) * LANES] = (
                    xp_ref[sl, pl.ds(s0 + k, nck, stride=pitch), :])
        return carry

    lax.fori_loop(0, chunk // SCAN_GROUP, to_chunk_major, 0)

    sub = lax.broadcasted_iota(jnp.int32, (nck, width), 0)
    for k in range(halo):
        tail = xq_ref[(chunk + k) * nck:(chunk + k + 1) * nck, :]
        xq_ref[k * nck:(k + 1) * nck, :] = jnp.where(sub == 0, 0.0, pltpu.roll(tail, 1, 0))
        head = xq_ref[(halo + k) * nck:(halo + k + 1) * nck, :]
        xq_ref[(halo + chunk + k) * nck:(halo + chunk + k + 1) * nck, :] = jnp.where(
            sub == nck - 1, 0.0, pltpu.roll(head, nck - 1, 0))

    for d in range(2):
        ch = (0.5 * RNN_C) * _softplus(-lam_ref[d:d + 1, :])
        reset_row = 0 if d == 0 else seq - 1

        def gates(bi, carry):
            base = pl.multiple_of(bi * GATE_ROWS, GATE_ROWS)
            xc = cb_ref[d:d + 1, :]
            for j in range(RNN_CONV_W):
                step = j if d == 0 else 2 * halo - j
                xc = xc + xq_ref[pl.ds(base + step * nck, GATE_ROWS), :] * cw_ref[d, j:j + 1, :]
            gt = jnp.dot(xc.astype(BF16), w_ref[d, 0], preferred_element_type=F32) + bias_ref[d, 0]
            tr = jnp.tanh(gt[:, :width])
            ti = jnp.tanh(gt[:, width:])
            nla = tr * ch + ch
            a = jnp.exp2(nla * (-LOG2E))
            m2 = jnp.tanh(nla) * (a * a + 1.0)
            mult = jnp.where(m2 > 0.0, m2 * lax.rsqrt(m2), 0.0)
            ixc = (0.5 * ti + 0.5) * xc
            bt = mult * ixc
            a_ref[d, pl.ds(base, GATE_ROWS), :] = a
            b_ref[d, pl.ds(base, GATE_ROWS), :] = bt

            @pl.when(bi == reset_row // GATE_ROWS)
            def _():
                lo = (reset_row % GATE_ROWS) // nck * nck
                fixed = jnp.where(sub == reset_row % nck, ixc[lo:lo + nck, :], bt[lo:lo + nck, :])
                b_ref[d, pl.ds(base + lo, nck), :] = fixed

            return carry

        lax.fori_loop(0, seq // GATE_ROWS, gates, 0)

    def scan(i, carry):
        state = list(carry)
        group_rows = SCAN_GROUP * nck
        for k in range(SCAN_GROUP):
            for d in range(2):
                h, acc = state[2 * d], state[2 * d + 1]
                if d == 0:
                    row = pl.multiple_of(i * group_rows, nck) + k * nck
                else:
                    row = pl.multiple_of((chunk - SCAN_GROUP) * nck - i * group_rows, nck) + (SCAN_GROUP - 1 - k) * nck
                av = a_ref[d, pl.ds(row, nck), :]
                h = av * h + b_ref[d, pl.ds(row, nck), :]
                acc = acc * av
                hl_ref[d, pl.ds(row, nck), :] = h
                al_ref[d, pl.ds(row, nck), :] = acc
                state[2 * d], state[2 * d + 1] = h, acc
        return tuple(state)

    init = (jnp.zeros((nck, width), F32), jnp.ones((nck, width), F32)) * 2
    ends = lax.fori_loop(0, chunk // SCAN_GROUP, scan, init)

    carries = []
    for d in range(2):
        h_end, a_end = ends[2 * d], ends[2 * d + 1]
        rows = [None] * nck
        cur = jnp.zeros((1, width), F32)
        for c in (range(nck) if d == 0 else range(nck - 1, -1, -1)):
            rows[c] = cur
            cur = a_end[c:c + 1, :] * cur + h_end[c:c + 1, :]
        carries.append(jnp.concatenate(rows, axis=0))

    def fold(i, carry):
        s0 = i * SCAN_GROUP
        row0 = pl.multiple_of(s0 * nck, nck)
        for k in range(SCAN_GROUP):
            rows = pl.ds(row0 + k * nck, nck)
            hv = (hl_ref[0, rows, :] + al_ref[0, rows, :] * carries[0]
                  + hl_ref[1, rows, :] + al_ref[1, rows, :] * carries[1])
            for sl in range(slabs):
                hs_ref[sl, pl.ds(s0 + k, nck, stride=pitch), :] = hv[:, sl * LANES:(sl + 1) * LANES]
        return carry

    lax.fori_loop(0, chunk // SCAN_GROUP, fold, 0)

    def finish(c, carry):
        r0 = pl.multiple_of(c * chunk, chunk)
        p0 = pl.multiple_of(c * pitch, SUBLANES)
        gate = _gelu_tanh(g_ref[pl.ds(r0, chunk), :])
        for sl in range(slabs):
            o_ref[pl.ds(r0, chunk), sl * LANES:(sl + 1) * LANES] = (
                hs_ref[sl, pl.ds(p0, chunk), :] * gate[:, sl * LANES:(sl + 1) * LANES]).astype(o_ref.dtype)
        return carry

    lax.fori_loop(0, nck, finish, 0)


def _rglru(u, conv_w, conv_b, w_gates, b_gates, lam, batch, seq_len):
    t, two_d = u.shape
    d = two_d // 2
    width = d // RNN_HEADS
    slabs = width // LANES
    chunk = seq_len // SUBLANES
    pitched = SUBLANES * (chunk + SCAN_PITCH_PAD)
    halo_rows = 2 * (RNN_CONV_W - 1) * SUBLANES
    return pl.pallas_call(
        _rglru_kernel,
        grid=(batch, RNN_HEADS),
        in_specs=[
            pl.BlockSpec((seq_len, width), lambda b, h: (b, h)),
            pl.BlockSpec((seq_len, width), lambda b, h: (b, RNN_HEADS + h)),
            pl.BlockSpec((2, RNN_CONV_W, width), lambda b, h: (0, 0, h)),
            pl.BlockSpec((2, width), lambda b, h: (0, h)),
            pl.BlockSpec((2, 1, width, 2 * width), lambda b, h: (0, h, 0, 0)),
            pl.BlockSpec((2, 1, 1, 2 * width), lambda b, h: (0, h, 0, 0)),
            pl.BlockSpec((2, width), lambda b, h: (0, h)),
        ],
        out_specs=pl.BlockSpec((seq_len, width), lambda b, h: (b, h)),
        out_shape=jax.ShapeDtypeStruct((t, d), BF16),
        scratch_shapes=[
            pltpu.VMEM((slabs, pitched, LANES), F32),
            pltpu.VMEM((seq_len + halo_rows, width), F32),
            pltpu.VMEM((2, seq_len, width), F32),
            pltpu.VMEM((2, seq_len, width), F32),
            pltpu.VMEM((2, seq_len, width), F32),
            pltpu.VMEM((2, seq_len, width), F32),
            pltpu.VMEM((slabs, pitched, LANES), F32),
        ],
        compiler_params=_params(("parallel", "parallel")),
        name="rglru",
    )(u, u, conv_w, conv_b, w_gates, b_gates, lam)


def _deinterleave_heads(w, n_heads):
    lead = w.shape[:-1]
    w = w.reshape(lead + (n_heads, HEAD_DIM // 2, 2))
    w = jnp.swapaxes(w, -1, -2)
    return w.reshape(lead + (n_heads * HEAD_DIM,))


def _rope_tables(seq_len):
    rows = seq_len // GRID_W
    row = jnp.repeat(jnp.arange(rows, dtype=F32), GRID_W)
    col = jnp.tile(jnp.arange(GRID_W, dtype=F32), rows)
    axis_dim = HEAD_DIM // 2
    omega = ROPE_THETA ** (-jnp.arange(0, axis_dim, 2, dtype=F32) / axis_dim)
    ang = jnp.concatenate([row[:, None] * omega[None], col[:, None] * omega[None]], axis=-1)
    c, s = jnp.cos(ang), jnp.sin(ang)
    return jnp.concatenate([c, c], axis=-1), jnp.concatenate([-s, s], axis=-1)


def _even_mixer(x, batch, seq_len, norm_g, w_in, short_w, short_b, f_w1, f_b1, f_w2, f_b2, f_w3, f_freq,
                hy_bias, q_g, k_g, w_out):
    d = x.shape[1]
    hy_d = hy_bias.shape[1]
    s0 = (HY_ORDER + 1) * hy_d
    nq = N_Q_HEADS * HEAD_DIM
    nk = N_KV_HEADS * HEAD_DIM
    g = norm_g.reshape(1, d)

    u_hy = _norm_proj(x, g, w_in[:, :s0].astype(BF16), F32, "hyena_in_proj")

    hid = LANES
    hpad = hid - f_w1.shape[1]
    zfeat = _position_features(seq_len, LANES)
    w1 = jnp.pad(f_w1, ((0, LANES - f_w1.shape[0]), (0, hpad)))
    b1 = jnp.pad(f_b1, (0, hpad)).reshape(1, hid)
    w2 = jnp.pad(f_w2, ((0, 0), (0, hpad), (0, hpad)))
    b2 = jnp.pad(f_b2, ((0, 0), (0, hpad))).reshape(HY_N_INNER, 1, hid)
    w3 = jnp.pad(f_w3, ((0, hpad), (0, 0)))
    freq = jnp.pad(f_freq, (0, hpad)).reshape(1, hid)
    max_decay = math.log(HY_DECAY_TARGET) / HY_FAST_DECAY
    min_decay = math.log(HY_DECAY_TARGET) / HY_SLOW_DECAY
    deltas = jnp.abs(jnp.linspace(min_decay, max_decay, hy_d, dtype=F32))
    deltas2 = jnp.tile(deltas, HY_ORDER).reshape(1, HY_ORDER * hy_d)
    ksum, kdiff = _hyena_filter_taps(zfeat, w1, b1, w2, b2, w3, freq, deltas2, seq_len, 256)
    p = _dft_matrix(seq_len)
    ka, kc, ny = _hyena_spectrum(p, ksum, kdiff, 1024)
    bias = hy_bias.reshape(1, HY_ORDER * hy_d)
    sb = short_b.reshape(1, s0)
    pt = p.T
    z1 = _long_conv(u_hy, 0, u_hy, 1, short_w, sb, pt, ka, kc, ny, bias, 0, True, batch, seq_len, 256, F32)
    y_hy = _long_conv(z1, 0, u_hy, 2, short_w, sb, pt, ka, kc, ny, bias, 1, False, batch, seq_len, 256, BF16)

    w_qkv = jnp.concatenate([
        _deinterleave_heads(w_in[:, s0:s0 + nq], N_Q_HEADS),
        _deinterleave_heads(w_in[:, s0 + nq:s0 + nq + nk], N_KV_HEADS),
        w_in[:, s0 + nq + nk:]], axis=1).astype(BF16)
    qg = _deinterleave_heads(q_g, 1) * (HEAD_DIM ** -0.5 * LOG2E)
    kg = _deinterleave_heads(k_g, 1)
    head_gain = jnp.concatenate([jnp.tile(qg, N_Q_HEADS), jnp.tile(kg, N_KV_HEADS),
                                 jnp.ones((nk,), F32)]).reshape(1, nq + 2 * nk)
    cos, sin = _rope_tables(seq_len)
    qkv = _qkv_proj(x, g, w_qkv, head_gain, cos, sin, seq_len)
    y_at = _attention(qkv, batch, seq_len, 512)

    return _out_proj(x, [y_hy, y_at], w_out.astype(BF16), "even_out_proj")


def _odd_mixer(x, batch, seq_len, norm_g, w_in, conv_w, conv_b, wa, ba, wx, bx, lam, w_out):
    d = x.shape[1]
    u = _norm_proj(x, norm_g.reshape(1, d), w_in.astype(BF16), F32, "rglru_in_proj")
    w_gates = (0.5 * jnp.concatenate([wa, wx], axis=-1)).astype(BF16)
    b_gates = 0.5 * jnp.concatenate([ba, bx], axis=-1)[:, :, None, :]
    y = _rglru(u, conv_w, conv_b, w_gates, b_gates, lam, batch, seq_len)
    return _out_proj(x, [y], w_out.astype(BF16), "rglru_out_proj")


def kernel(x_prompt, x_sample, ffn_norm, ffn_w_gate, ffn_w_up, ffn_w_down, even_norm, even_w_in, hy_short_w, hy_short_b, hy_filt_w1, hy_filt_b1, hy_filt_w2, hy_filt_b2, hy_filt_w3, hy_filt_freq, hy_bias, q_norm, k_norm, even_w_out, odd_norm, odd_w_in, rg_conv_w, rg_conv_b, rg_wa, rg_ba, rg_wx, rg_bx, rg_lambda, odd_w_out):
    bp, seq_len, d = x_prompt.shape
    bs = x_sample.shape[0]
    assert x_sample.shape[1:] == (seq_len, d)
    batch = bp + bs
    rows_p = bp * seq_len

    depth = ffn_norm.shape[0]
    ff = ffn_w_gate.shape[-1]
    fp = -(-ff // TF) * TF
    wg = _cast_pad(ffn_w_gate.reshape(depth * 2, d, ff), d, fp, 1024, TF)
    wu = _cast_pad(ffn_w_up.reshape(depth * 2, d, ff), d, fp, 1024, TF)
    wd = _cast_pad(ffn_w_down.reshape(depth * 2, ff, d), fp, d, TF, d)
    fg = ffn_norm.reshape(depth * 2, 1, d)
    rows_s = bs * seq_len
    total = rows_p + rows_s

    x = None
    for layer in range(depth):
        if layer == 0:
            x = _ffn_half(x_prompt.reshape(rows_p, d), fg, 0, wg, wu, wd, out_rows=total)
            x = _ffn_half(x_sample.reshape(rows_s, d), fg, 0, wg, wu, wd, out_rows=total, out_row0=rows_p, dest=x)
        else:
            x = _ffn_half(x, fg, 2 * layer, wg, wu, wd)
        j = layer // 2
        if layer % 2 == 0:
            x = _even_mixer(x, batch, seq_len, even_norm[j], even_w_in[j], hy_short_w[j], hy_short_b[j],
                            hy_filt_w1[j], hy_filt_b1[j], hy_filt_w2[j], hy_filt_b2[j], hy_filt_w3[j],
                            hy_filt_freq[j], hy_bias[j], q_norm[j], k_norm[j], even_w_out[j])
        else:
            x = _odd_mixer(x, batch, seq_len, odd_norm[j], odd_w_in[j], rg_conv_w[j], rg_conv_b[j],
                           rg_wa[j], rg_ba[j], rg_wx[j], rg_bx[j], rg_lambda[j], odd_w_out[j])
        if layer == depth - 1:
            y_prompt = _ffn_half(x, fg, 2 * layer + 1, wg, wu, wd, rows=rows_p)
            y_sample = _ffn_half(x, fg, 2 * layer + 1, wg, wu, wd, rows=rows_s, x_row0=rows_p)
        else:
            x = _ffn_half(x, fg, 2 * layer + 1, wg, wu, wd)

    return (y_prompt.reshape(bp, seq_len, d), y_sample.reshape(bs, seq_len, d))
```

```python
import functools
import math

import jax
import jax.numpy as jnp
from jax import lax
from jax.experimental import pallas as pl
from jax.experimental.pallas import tpu as pltpu

F32 = jnp.float32
BF16 = jnp.bfloat16

NORM_EPS = 1e-6
GRID_W = 64
HEAD_DIM = 128
N_Q_HEADS = 8
N_KV_HEADS = 2
GQA_GROUP = N_Q_HEADS // N_KV_HEADS
ROPE_THETA = 10000.0
HY_ORDER = 2
HY_SHORT_W = 3
HY_EMB_DIM = 33
HY_BANDS = (HY_EMB_DIM - 1) // 2
HY_N_INNER = 2
HY_FAST_DECAY = 0.3
HY_SLOW_DECAY = 1.5
HY_DECAY_TARGET = 1e-2
RNN_HEADS = 8
RNN_CONV_W = 4
RNN_C = 8.0
LOG2E = 1.4426950408889634

LANES = 128
SUBLANES = 8
VMEM_LIMIT = 56 * 1024 * 1024

TM = 512
TM_FFN = 1024
TF = 512
TN = 1024
ATT_KC = 256
FB = 256
CONV_ROWS = 64
GATE_ROWS = 512
SCAN_GROUP = 8
SCAN_PITCH_PAD = 8


def _params(sem, vmem=VMEM_LIMIT):
    return pltpu.CompilerParams(dimension_semantics=sem, vmem_limit_bytes=vmem)


def _resident(shape):
    return pl.BlockSpec(shape, lambda *_: (0,) * len(shape), pipeline_mode=pl.Buffered(1))


def _rms(x, g):
    ms = jnp.mean(x * x, axis=-1, keepdims=True)
    return x * lax.rsqrt(ms + NORM_EPS) * g


def _cast_pad_kernel(w_ref, o_ref, *, rows, cols):
    br, bc = o_ref.shape
    r = lax.broadcasted_iota(jnp.int32, (br, bc), 0) + pl.program_id(1) * br
    c = lax.broadcasted_iota(jnp.int32, (br, bc), 1) + pl.program_id(2) * bc
    w = jnp.where(r < rows, jnp.where(c < cols, w_ref[...], 0.0), 0.0)
    o_ref[...] = w.astype(o_ref.dtype)


def _cast_pad(w, rows_out, cols_out, br, bc):
    n, rows, cols = w.shape
    return pl.pallas_call(
        functools.partial(_cast_pad_kernel, rows=rows, cols=cols),
        grid=(n, rows_out // br, cols_out // bc),
        in_specs=[pl.BlockSpec((None, br, bc), lambda k, i, j: (k, i, j))],
        out_specs=pl.BlockSpec((None, br, bc), lambda k, i, j: (k, i, j)),
        out_shape=jax.ShapeDtypeStruct((n, rows_out, cols_out), BF16),
        compiler_params=_params(("parallel", "parallel", "parallel")),
        name="cast_pad",
    )(w)


def _ffn_kernel(x_ref, g_ref, wg_ref, wu_ref, wd_ref, *rest):
    o_ref, n_ref = rest[-2:]
    j = pl.program_id(1)

    def contribution():
        n = n_ref[...]
        hg = jnp.dot(n, wg_ref[...], preferred_element_type=F32)
        hu = jnp.dot(n, wu_ref[...], preferred_element_type=F32)
        h = ((hg * jax.nn.sigmoid(hg)) * (hu * 0.5)).astype(BF16)
        return jnp.dot(h, wd_ref[...], preferred_element_type=F32)

    @pl.when(j == 0)
    def _():
        n_ref[...] = _rms(x_ref[...], g_ref[...]).astype(BF16)
        o_ref[...] = x_ref[...] + contribution()

    @pl.when(j > 0)
    def _():
        o_ref[...] += contribution()


def _ffn_half(x, g, w_idx, wg, wu, wd, *, rows=None, x_row0=0, out_rows=None, out_row0=0, dest=None):
    d = x.shape[1]
    fp = wg.shape[2]
    rows = x.shape[0] if rows is None else rows
    out_rows = rows if out_rows is None else out_rows
    xb = x_row0 // TM_FFN
    ob = out_row0 // TM_FFN
    in_specs = [
        pl.BlockSpec((TM_FFN, d), lambda i, j: (i + xb, 0)),
        pl.BlockSpec((None, 1, d), lambda i, j: (w_idx, 0, 0)),
        pl.BlockSpec((None, d, TF), lambda i, j: (w_idx, 0, j)),
        pl.BlockSpec((None, d, TF), lambda i, j: (w_idx, 0, j)),
        pl.BlockSpec((None, TF, d), lambda i, j: (w_idx, j, 0)),
    ]
    args = [x, g, wg, wu, wd]
    aliases = {}
    if dest is not None:
        in_specs.append(pl.BlockSpec(memory_space=pl.ANY))
        args.append(dest)
        aliases = {len(args) - 1: 0}
    return pl.pallas_call(
        _ffn_kernel,
        grid=(rows // TM_FFN, fp // TF),
        in_specs=in_specs,
        out_specs=pl.BlockSpec((TM_FFN, d), lambda i, j: (i + ob, 0)),
        out_shape=jax.ShapeDtypeStruct((out_rows, d), F32),
        scratch_shapes=[pltpu.VMEM((TM_FFN, d), BF16)],
        input_output_aliases=aliases,
        compiler_params=_params(("parallel", "arbitrary")),
        name="ffn_half",
    )(*args)


def _proj_kernel(x_ref, g_ref, w_ref, o_ref):
    n = _rms(x_ref[...], g_ref[...]).astype(BF16)
    for c in range(0, w_ref.shape[1], TN):
        o_ref[:, c:c + TN] = jnp.dot(n, w_ref[:, c:c + TN], preferred_element_type=F32).astype(o_ref.dtype)


def _norm_proj(x, g, w, out_dtype, name):
    t, d = x.shape
    n = w.shape[1]
    return pl.pallas_call(
        _proj_kernel,
        grid=(t // TM,),
        in_specs=[
            pl.BlockSpec((TM, d), lambda i: (i, 0)),
            _resident((1, d)),
            _resident((d, n)),
        ],
        out_specs=pl.BlockSpec((TM, n), lambda i: (i, 0)),
        out_shape=jax.ShapeDtypeStruct((t, n), out_dtype),
        compiler_params=_params(("parallel",)),
        name=name,
    )(x, g, w)


def _qkv_kernel(x_ref, g_ref, w_ref, hg_ref, cos_ref, sin_ref, o_ref, *, n_rope):
    n = _rms(x_ref[...], g_ref[...]).astype(BF16)
    u = jnp.dot(n, w_ref[...], preferred_element_type=F32)
    c = cos_ref[...]
    s = sin_ref[...]
    for h in range(u.shape[1] // HEAD_DIM):
        sl = slice(h * HEAD_DIM, (h + 1) * HEAD_DIM)
        uh = u[:, sl]
        if h < n_rope:
            ms = jnp.mean(uh * uh, axis=-1, keepdims=True)
            uh = uh * lax.rsqrt(ms + NORM_EPS) * hg_ref[:, sl]
            uh = uh * c + pltpu.roll(uh, HEAD_DIM // 2, 1) * s
        o_ref[:, sl] = uh.astype(o_ref.dtype)


def _qkv_proj(x, g, w, head_gain, cos, sin, seq_len):
    t, d = x.shape
    n = w.shape[1]
    tiles_per_seq = seq_len // TM
    n_rope = N_Q_HEADS + N_KV_HEADS
    return pl.pallas_call(
        functools.partial(_qkv_kernel, n_rope=n_rope),
        grid=(t // TM,),
        in_specs=[
            pl.BlockSpec((TM, d), lambda i: (i, 0)),
            _resident((1, d)),
            _resident((d, n)),
            _resident((1, n)),
            pl.BlockSpec((TM, HEAD_DIM), lambda i: (i % tiles_per_seq, 0)),
            pl.BlockSpec((TM, HEAD_DIM), lambda i: (i % tiles_per_seq, 0)),
        ],
        out_specs=pl.BlockSpec((TM, n), lambda i: (i, 0)),
        out_shape=jax.ShapeDtypeStruct((t, n), BF16),
        compiler_params=_params(("parallel",)),
        name="qkv_proj",
    )(x, g, w, head_gain, cos, sin)


def _outproj_kernel(*refs, n_y):
    r_ref = refs[0]
    y_refs = refs[1:1 + n_y]
    w_ref = refs[1 + n_y]
    o_ref = refs[2 + n_y]
    ys = [y_ref[...] for y_ref in y_refs]
    for c in range(0, o_ref.shape[1], TN):
        acc = r_ref[:, c:c + TN]
        k0 = 0
        for y in ys:
            k = y.shape[1]
            acc = acc + jnp.dot(y, w_ref[k0:k0 + k, c:c + TN], preferred_element_type=F32)
            k0 += k
        o_ref[:, c:c + TN] = acc


def _out_proj(res, ys, w, name):
    t, d = res.shape
    return pl.pallas_call(
        functools.partial(_outproj_kernel, n_y=len(ys)),
        grid=(t // TM,),
        in_specs=[pl.BlockSpec((TM, d), lambda i: (i, 0))]
        + [pl.BlockSpec((TM, y.shape[1]), lambda i: (i, 0)) for y in ys]
        + [_resident(w.shape)],
        out_specs=pl.BlockSpec((TM, d), lambda i: (i, 0)),
        out_shape=jax.ShapeDtypeStruct((t, d), F32),
        compiler_params=_params(("parallel",)),
        name=name,
    )(res, *ys, w)


def _attn_kernel(q_ref, k_ref, v_ref, o_ref, st_ref):
    seq = k_ref.shape[0]
    n_heads = q_ref.shape[1] // HEAD_DIM
    nt = (((1,), (1,)), ((), ()))
    tn = (((0,), (0,)), ((), ()))
    chunks = [slice(c * ATT_KC, (c + 1) * ATT_KC) for c in range(seq // ATT_KC)]

    def scores(g, rows, slot, m):
        s = lax.dot_general(k_ref[rows, :], q_ref[:, g * HEAD_DIM:(g + 1) * HEAD_DIM], nt,
                            preferred_element_type=F32)
        st_ref[slot, rows, :] = s
        cm = jnp.max(s, axis=0, keepdims=True)
        return cm if m is None else jnp.maximum(m, cm)

    m = None
    for rows in chunks:
        m = scores(0, rows, 0, m)
    for g in range(n_heads):
        cur = g % 2
        acc = l = m_next = None
        for rows in chunks:
            if g + 1 < n_heads:
                m_next = scores(g + 1, rows, 1 - cur, m_next)
            p = jnp.exp2(st_ref[cur, rows, :] - m)
            ps = jnp.sum(p, axis=0, keepdims=True)
            pv = lax.dot_general(v_ref[rows, :], p.astype(BF16), tn, preferred_element_type=F32)
            l = ps if l is None else l + ps
            acc = pv if acc is None else acc + pv
        o_ref[:, g * HEAD_DIM:(g + 1) * HEAD_DIM] = (acc / l).T.astype(o_ref.dtype)
        m = m_next


def _attention(qkv, batch, seq_len, tq):
    t = qkv.shape[0]
    gw = GQA_GROUP * HEAD_DIM
    nq = seq_len // tq
    k_off = N_Q_HEADS
    v_off = N_Q_HEADS + N_KV_HEADS
    return pl.pallas_call(
        _attn_kernel,
        grid=(batch, N_KV_HEADS, nq),
        in_specs=[
            pl.BlockSpec((tq, gw), lambda b, h, i: (b * nq + i, h)),
            pl.BlockSpec((seq_len, HEAD_DIM), lambda b, h, i: (b, k_off + h)),
            pl.BlockSpec((seq_len, HEAD_DIM), lambda b, h, i: (b, v_off + h)),
        ],
        out_specs=pl.BlockSpec((tq, gw), lambda b, h, i: (b * nq + i, h)),
        out_shape=jax.ShapeDtypeStruct((t, N_Q_HEADS * HEAD_DIM), BF16),
        scratch_shapes=[pltpu.VMEM((2, seq_len, tq), F32)],
        compiler_params=_params(("parallel", "parallel", "arbitrary")),
        name="attention",
    )(qkv, qkv, qkv)


def _fill_padded(src_ref, pad_ref):
    seq = src_ref.shape[0]
    zeros = jnp.zeros((SUBLANES, pad_ref.shape[1]), F32)
    pad_ref[0:SUBLANES, :] = zeros
    pad_ref[seq + SUBLANES:seq + 2 * SUBLANES, :] = zeros
    pad_ref[SUBLANES:seq + SUBLANES, :] = src_ref[...]


def _short_conv_rows(pad_ref, w, b, r0):
    n = CONV_ROWS + 2 * SUBLANES
    win = pad_ref[pl.ds(r0, n), :]
    taps = (pltpu.roll(win, 1, 0), win, pltpu.roll(win, n - 1, 0))
    y = b
    for j in range(HY_SHORT_W):
        y = y + taps[j][SUBLANES:SUBLANES + CONV_ROWS, :] * w[j:j + 1, :]
    return y


def _filter_kernel(z_ref, w1_ref, b1_ref, w2_ref, b2_ref, w3_ref, fr_ref, dl_ref, ks_ref, kd_ref, *, seq_len):
    hi = lax.Precision.HIGHEST
    fr = fr_ref[...]
    h = jnp.sin(fr * (jnp.dot(z_ref[...], w1_ref[...], precision=hi, preferred_element_type=F32) + b1_ref[...]))
    for i in range(HY_N_INNER):
        h = jnp.sin(fr * (jnp.dot(h, w2_ref[i], precision=hi, preferred_element_type=F32) + b2_ref[i]))
    k = jnp.dot(h, w3_ref[...], precision=hi, preferred_element_type=F32)
    tl, half = ks_ref.shape
    pos = lax.broadcasted_iota(jnp.int32, (tl, half), 0) + pl.program_id(0) * tl
    tnorm = pos.astype(F32) / (seq_len - 1)
    decay = jnp.exp(-tnorm * dl_ref[...])
    kf = k[:, :half] * decay
    kb = jnp.where(pos == 0, 0.0, k[:, half:] * decay)
    ks_ref[...] = (kf + kb).astype(ks_ref.dtype)
    kd_ref[...] = (kf - kb).astype(kd_ref.dtype)


def _hyena_filter_taps(zfeat, w1, b1, w2, b2, w3, freq, deltas2, seq_len, tl):
    half = deltas2.shape[1]
    return pl.pallas_call(
        functools.partial(_filter_kernel, seq_len=seq_len),
        grid=(seq_len // tl,),
        in_specs=[
            pl.BlockSpec((tl, zfeat.shape[1]), lambda i: (i, 0)),
            _resident(w1.shape),
            _resident(b1.shape),
            _resident(w2.shape),
            _resident(b2.shape),
            _resident(w3.shape),
            _resident(freq.shape),
            _resident(deltas2.shape),
        ],
        out_specs=[pl.BlockSpec((tl, half), lambda i: (i, 0))] * 2,
        out_shape=[jax.ShapeDtypeStruct((seq_len, half), BF16)] * 2,
        compiler_params=_params(("parallel",)),
        name="hyena_filter_taps",
    )(zfeat, w1, b1, w2, b2, w3, freq, deltas2)


def _spectrum_kernel(p_ref, ks_ref, kd_ref, ka_ref, kc_ref, ny_ref, *, period):
    fb = ka_ref.shape[0]
    ks = ks_ref[...]
    re = jnp.dot(p_ref[0:fb, :], ks, preferred_element_type=F32)
    im = jnp.dot(p_ref[fb:2 * fb, :], kd_ref[...], preferred_element_type=F32)
    ny = jnp.dot(p_ref[fb:fb + SUBLANES, :], ks, preferred_element_type=F32)
    is_dc = (lax.broadcasted_iota(jnp.int32, re.shape, 0) + pl.program_id(1) * fb) == 0
    ka_ref[...] = re * jnp.where(is_dc, 1.0 / period, 2.0 / period)
    kc_ref[...] = jnp.where(is_dc, 0.0, im * (2.0 / period))
    ny_ref[...] = ny * (1.0 / period)


def _hyena_spectrum(p, ksum, kdiff, tcs):
    seq_len, cols = ksum.shape
    nf = p.shape[0] // (2 * FB)
    return pl.pallas_call(
        functools.partial(_spectrum_kernel, period=2 * seq_len),
        grid=(cols // tcs, nf),
        in_specs=[
            pl.BlockSpec((2 * FB, seq_len), lambda c, f: (f, 0)),
            pl.BlockSpec((seq_len, tcs), lambda c, f: (0, c)),
            pl.BlockSpec((seq_len, tcs), lambda c, f: (0, c)),
        ],
        out_specs=[
            pl.BlockSpec((FB, tcs), lambda c, f: (f, c)),
            pl.BlockSpec((FB, tcs), lambda c, f: (f, c)),
            pl.BlockSpec((SUBLANES, tcs), lambda c, f: (f, c)),
        ],
        out_shape=[
            jax.ShapeDtypeStruct((nf * FB, cols), F32),
            jax.ShapeDtypeStruct((nf * FB, cols), F32),
            jax.ShapeDtypeStruct((nf * SUBLANES, cols), F32),
        ],
        compiler_params=_params(("parallel", "parallel")),
        name="hyena_filter_spectrum",
    )(p, ksum, kdiff)


def _longconv_kernel(z_ref, gate_ref, swz_ref, sbz_ref, swg_ref, sbg_ref, pt_ref, ka_ref, kc_ref, ny_ref,
                     bias_ref, o_ref, pad_ref, zc_ref, zb_ref, y_ref, *, conv_z):
    seq = z_ref.shape[0]
    nf = seq // FB

    if conv_z:
        _fill_padded(z_ref, pad_ref)
        w = swz_ref[...]
        b = sbz_ref[...]

        def body(i, carry):
            r0 = pl.multiple_of(i * CONV_ROWS, CONV_ROWS)
            y = _short_conv_rows(pad_ref, w, b, r0)
            zc_ref[pl.ds(r0, CONV_ROWS), :] = y
            zb_ref[pl.ds(r0, CONV_ROWS), :] = y.astype(BF16)
            return carry

        lax.fori_loop(0, seq // CONV_ROWS, body, 0)
    else:
        zb_ref[...] = z_ref[...].astype(BF16)
    _fill_padded(gate_ref, pad_ref)

    tn = (((0,), (0,)), ((), ()))
    for f in range(nf):
        cols = slice(f * 2 * FB, (f + 1) * 2 * FB)
        spec = lax.dot_general(pt_ref[:, cols], zb_ref[...], tn, preferred_element_type=F32)
        xre = spec[:FB]
        xim = spec[FB:]
        ka = ka_ref[f * FB:(f + 1) * FB, :]
        kc = kc_ref[f * FB:(f + 1) * FB, :]
        kd = ka
        if f == 0:
            is_dc = lax.broadcasted_iota(jnp.int32, ka.shape, 0) == 0
            kd = jnp.where(is_dc, ny_ref[0:1, :], ka)
        y_ref[f * 2 * FB:f * 2 * FB + FB, :] = (xre * ka - xim * kc).astype(BF16)
        y_ref[f * 2 * FB + FB:(f + 1) * 2 * FB, :] = (xre * kc + xim * kd).astype(BF16)

    wg = swg_ref[...]
    bg = sbg_ref[...]
    bias = bias_ref[...]
    zsrc = zc_ref if conv_z else z_ref
    for r in range(nf):
        conv = jnp.dot(pt_ref[r * FB:(r + 1) * FB, :], y_ref[...], preferred_element_type=F32)
        for k in range(FB // CONV_ROWS):
            r0 = r * FB + k * CONV_ROWS
            g = _short_conv_rows(pad_ref, wg, bg, r0)
            y = conv[k * CONV_ROWS:(k + 1) * CONV_ROWS, :]
            o_ref[r0:r0 + CONV_ROWS, :] = (g * (y + zsrc[r0:r0 + CONV_ROWS, :] * bias)).astype(o_ref.dtype)


def _long_conv(z, z_col, gate, gate_col, short_w, short_b, pt, ka, kc, ny, bias, order, conv_z,
               batch, seq_len, tc, out_dtype):
    c = bias.shape[1] // HY_ORDER
    t = z.shape[0]
    nc = c // tc
    zc_col = z_col if conv_z else 0
    return pl.pallas_call(
        functools.partial(_longconv_kernel, conv_z=conv_z),
        grid=(nc, batch),
        in_specs=[
            pl.BlockSpec((seq_len, tc), lambda j, b: (b, z_col * nc + j)),
            pl.BlockSpec((seq_len, tc), lambda j, b: (b, gate_col * nc + j)),
            pl.BlockSpec((HY_SHORT_W, tc), lambda j, b: (0, zc_col * nc + j)),
            pl.BlockSpec((1, tc), lambda j, b: (0, zc_col * nc + j)),
            pl.BlockSpec((HY_SHORT_W, tc), lambda j, b: (0, gate_col * nc + j)),
            pl.BlockSpec((1, tc), lambda j, b: (0, gate_col * nc + j)),
            _resident(pt.shape),
            pl.BlockSpec((seq_len, tc), lambda j, b: (0, order * nc + j)),
            pl.BlockSpec((seq_len, tc), lambda j, b: (0, order * nc + j)),
            pl.BlockSpec((SUBLANES, tc), lambda j, b: (0, order * nc + j)),
            pl.BlockSpec((1, tc), lambda j, b: (0, order * nc + j)),
        ],
        out_specs=pl.BlockSpec((seq_len, tc), lambda j, b: (b, j)),
        out_shape=jax.ShapeDtypeStruct((t, c), out_dtype),
        scratch_shapes=[
            pltpu.VMEM((seq_len + 2 * SUBLANES, tc), F32),
            pltpu.VMEM((seq_len, tc) if conv_z else (SUBLANES, LANES), F32),
            pltpu.VMEM((seq_len, tc), BF16),
            pltpu.VMEM((2 * seq_len, tc), BF16),
        ],
        compiler_params=_params(("parallel", "parallel")),
        name="hyena_long_conv",
    )(z, gate, short_w, short_b, short_w, short_b, pt, ka, kc, ny, bias)


def _dft_matrix(seq_len):
    period = 2 * seq_len
    split = 64
    f = jnp.arange(seq_len, dtype=jnp.int32)[:, None]

    def trig(tvals):
        ang = ((f * tvals[None, :]) % period).astype(F32) * (2.0 * math.pi / period)
        return jnp.cos(ang), jnp.sin(ang)

    ca, sa = trig(jnp.arange(seq_len // split, dtype=jnp.int32) * split)
    cb, sb = trig(jnp.arange(split, dtype=jnp.int32))
    re = (ca[:, :, None] * cb[:, None, :] - sa[:, :, None] * sb[:, None, :]).reshape(seq_len, seq_len)
    im = -(sa[:, :, None] * cb[:, None, :] + ca[:, :, None] * sb[:, None, :]).reshape(seq_len, seq_len)
    nyq = jnp.where(jnp.arange(seq_len) % 2 == 0, 1.0, -1.0).astype(F32)
    im = im.at[0].set(nyq)
    nf = seq_len // FB
    p = jnp.stack([re.reshape(nf, FB, seq_len), im.reshape(nf, FB, seq_len)], axis=1)
    p = p.reshape(2 * seq_len, seq_len).astype(BF16)
    ret = (ca.T[:, None, :] * cb.T[None, :, :] - sa.T[:, None, :] * sb.T[None, :, :]).reshape(seq_len, seq_len)
    imt = -(sa.T[:, None, :] * cb.T[None, :, :] + ca.T[:, None, :] * sb.T[None, :, :]).reshape(seq_len, seq_len)
    imt = imt.at[:, 0].set(nyq)
    pt = jnp.stack([ret.reshape(seq_len, nf, FB), imt.reshape(seq_len, nf, FB)], axis=2)
    return p, pt.reshape(seq_len, 2 * seq_len).astype(BF16)


def _position_features(seq_len, width):
    pos = jnp.arange(seq_len, dtype=F32)
    t = pos / (seq_len - 1)
    w = 2.0 * math.pi * pos / seq_len
    f = jnp.linspace(1e-4, HY_BANDS - 1, HY_BANDS, dtype=F32)
    fw = w[:, None] * f[None, :]
    z = jnp.concatenate([t[:, None], jnp.cos(fw), -jnp.sin(fw)], axis=-1)
    return jnp.pad(z, ((0, 0), (0, width - z.shape[1])))


def _softplus(x):
    return jnp.maximum(x, 0.0) + jnp.log1p(jnp.exp(-jnp.abs(x)))


def _gelu_tanh(x):
    return 0.5 * x * (1.0 + jnp.tanh(math.sqrt(2.0 / math.pi) * (x + 0.044715 * (x * x * x))))


def _rglru_kernel(g_ref, x_ref, cw_ref, cb_ref, w_ref, bias_ref, lam_ref, o_ref,
                  xp_ref, xq_ref, a_ref, b_ref, hl_ref, al_ref, hs_ref):
    seq, width = x_ref.shape
    nck = SUBLANES
    chunk = seq // nck
    pitch = chunk + SCAN_PITCH_PAD
    slabs = width // LANES
    halo = RNN_CONV_W - 1

    for c in range(nck):
        for sl in range(slabs):
            xp_ref[sl, c * pitch:c * pitch + chunk, :] = x_ref[c * chunk:(c + 1) * chunk, sl * LANES:(sl + 1) * LANES]

    def to_chunk_major(i, carry):
        s0 = i * SCAN_GROUP
        row0 = pl.multiple_of((s0 + halo) * nck, nck)
        for k in range(SCAN_GROUP):
            for sl in range(slabs):
                xq_ref[pl.ds(row0 + k * nck, nck), sl * LANES:(sl + 1) * LANES] = (
                    xp_ref[sl, pl.ds(s0 + k, nck, stride=pitch), :])
        return carry

    lax.fori_loop(0, chunk // SCAN_GROUP, to_chunk_major, 0)

    sub = lax.broadcasted_iota(jnp.int32, (nck, width), 0)
    for k in range(halo):
        tail = xq_ref[(chunk + k) * nck:(chunk + k + 1) * nck, :]
        xq_ref[k * nck:(k + 1) * nck, :] = jnp.where(sub == 0, 0.0, pltpu.roll(tail, 1, 0))
        head = xq_ref[(halo + k) * nck:(halo + k + 1) * nck, :]
        xq_ref[(halo + chunk + k) * nck:(halo + chunk + k + 1) * nck, :] = jnp.where(
            sub == nck - 1, 0.0, pltpu.roll(head, nck - 1, 0))

    for d in range(2):
        ch = (0.5 * RNN_C) * _softplus(-lam_ref[d:d + 1, :])
        reset_row = 0 if d == 0 else seq - 1

        def gates(bi, carry):
            base = pl.multiple_of(bi * GATE_ROWS, GATE_ROWS)
            xc = cb_ref[d:d + 1, :]
            for j in range(RNN_CONV_W):
                step = j if d == 0 else 2 * halo - j
                xc = xc + xq_ref[pl.ds(base + step * nck, GATE_ROWS), :] * cw_ref[d, j:j + 1, :]
            gt = jnp.dot(xc.astype(BF16), w_ref[d, 0], preferred_element_type=F32) + bias_ref[d, 0]
            tr = jnp.tanh(gt[:, :width])
            ti = jnp.tanh(gt[:, width:])
            nla = tr * ch + ch
            a = jnp.exp2(nla * (-LOG2E))
            m2 = jnp.tanh(nla) * (a * a + 1.0)
            mult = jnp.where(m2 > 0.0, m2 * lax.rsqrt(m2), 0.0)
            ixc = (0.5 * ti + 0.5) * xc
            bt = mult * ixc
            a_ref[d, pl.ds(base, GATE_ROWS), :] = a
            b_ref[d, pl.ds(base, GATE_ROWS), :] = bt

            @pl.when(bi == reset_row // GATE_ROWS)
            def _():
                lo = (reset_row % GATE_ROWS) // nck * nck
                fixed = jnp.where(sub == reset_row % nck, ixc[lo:lo + nck, :], bt[lo:lo + nck, :])
                b_ref[d, pl.ds(base + lo, nck), :] = fixed

            return carry

        lax.fori_loop(0, seq // GATE_ROWS, gates, 0)

    def scan(i, carry):
        state = list(carry)
        group_rows = SCAN_GROUP * nck
        for k in range(SCAN_GROUP):
            for d in range(2):
                h, acc = state[2 * d], state[2 * d + 1]
                if d == 0:
                    row = pl.multiple_of(i * group_rows, nck) + k * nck
                else:
                    row = pl.multiple_of((chunk - SCAN_GROUP) * nck - i * group_rows, nck) + (SCAN_GROUP - 1 - k) * nck
                av = a_ref[d, pl.ds(row, nck), :]
                h = av * h + b_ref[d, pl.ds(row, nck), :]
                acc = acc * av
                hl_ref[d, pl.ds(row, nck), :] = h
                al_ref[d, pl.ds(row, nck), :] = acc
                state[2 * d], state[2 * d + 1] = h, acc
        return tuple(state)

    init = (jnp.zeros((nck, width), F32), jnp.ones((nck, width), F32)) * 2
    ends = lax.fori_loop(0, chunk // SCAN_GROUP, scan, init)

    carries = []
    for d in range(2):
        h_end, a_end = ends[2 * d], ends[2 * d + 1]
        rows = [None] * nck
        cur = jnp.zeros((1, width), F32)
        for c in (range(nck) if d == 0 else range(nck - 1, -1, -1)):
            rows[c] = cur
            cur = a_end[c:c + 1, :] * cur + h_end[c:c + 1, :]
        carries.append(jnp.concatenate(rows, axis=0))

    def fold(i, carry):
        s0 = i * SCAN_GROUP
        row0 = pl.multiple_of(s0 * nck, nck)
        for k in range(SCAN_GROUP):
            rows = pl.ds(row0 + k * nck, nck)
            hv = (hl_ref[0, rows, :] + al_ref[0, rows, :] * carries[0]
                  + hl_ref[1, rows, :] + al_ref[1, rows, :] * carries[1])
            for sl in range(slabs):
                hs_ref[sl, pl.ds(s0 + k, nck, stride=pitch), :] = hv[:, sl * LANES:(sl + 1) * LANES]
        return carry

    lax.fori_loop(0, chunk // SCAN_GROUP, fold, 0)

    def finish(c, carry):
        r0 = pl.multiple_of(c * chunk, chunk)
        p0 = pl.multiple_of(c * pitch, SUBLANES)
        gate = _gelu_tanh(g_ref[pl.ds(r0, chunk), :])
        for sl in range(slabs):
            o_ref[pl.ds(r0, chunk), sl * LANES:(sl + 1) * LANES] = (
                hs_ref[sl, pl.ds(p0, chunk), :] * gate[:, sl * LANES:(sl + 1) * LANES]).astype(o_ref.dtype)
        return carry

    lax.fori_loop(0, nck, finish, 0)


def _rglru(u, conv_w, conv_b, w_gates, b_gates, lam, batch, seq_len):
    t, two_d = u.shape
    d = two_d // 2
    width = d // RNN_HEADS
    slabs = width // LANES
    chunk = seq_len // SUBLANES
    pitched = SUBLANES * (chunk + SCAN_PITCH_PAD)
    halo_rows = 2 * (RNN_CONV_W - 1) * SUBLANES
    return pl.pallas_call(
        _rglru_kernel,
        grid=(batch, RNN_HEADS),
        in_specs=[
            pl.BlockSpec((seq_len, width), lambda b, h: (b, h)),
            pl.BlockSpec((seq_len, width), lambda b, h: (b, RNN_HEADS + h)),
            pl.BlockSpec((2, RNN_CONV_W, width), lambda b, h: (0, 0, h)),
            pl.BlockSpec((2, width), lambda b, h: (0, h)),
            pl.BlockSpec((2, 1, width, 2 * width), lambda b, h: (0, h, 0, 0)),
            pl.BlockSpec((2, 1, 1, 2 * width), lambda b, h: (0, h, 0, 0)),
            pl.BlockSpec((2, width), lambda b, h: (0, h)),
        ],
        out_specs=pl.BlockSpec((seq_len, width), lambda b, h: (b, h)),
        out_shape=jax.ShapeDtypeStruct((t, d), BF16),
        scratch_shapes=[
            pltpu.VMEM((slabs, pitched, LANES), F32),
            pltpu.VMEM((seq_len + halo_rows, width), F32),
            pltpu.VMEM((2, seq_len, width), F32),
            pltpu.VMEM((2, seq_len, width), F32),
            pltpu.VMEM((2, seq_len, width), F32),
            pltpu.VMEM((2, seq_len, width), F32),
            pltpu.VMEM((slabs, pitched, LANES), F32),
        ],
        compiler_params=_params(("parallel", "parallel")),
        name="rglru",
    )(u, u, conv_w, conv_b, w_gates, b_gates, lam)


def _deinterleave_heads(w, n_heads):
    lead = w.shape[:-1]
    w = w.reshape(lead + (n_heads, HEAD_DIM // 2, 2))
    w = jnp.swapaxes(w, -1, -2)
    return w.reshape(lead + (n_heads * HEAD_DIM,))


def _rope_tables(seq_len):
    rows = seq_len // GRID_W
    row = jnp.repeat(jnp.arange(rows, dtype=F32), GRID_W)
    col = jnp.tile(jnp.arange(GRID_W, dtype=F32), rows)
    axis_dim = HEAD_DIM // 2
    omega = ROPE_THETA ** (-jnp.arange(0, axis_dim, 2, dtype=F32) / axis_dim)
    ang = jnp.concatenate([row[:, None] * omega[None], col[:, None] * omega[None]], axis=-1)
    c, s = jnp.cos(ang), jnp.sin(ang)
    return jnp.concatenate([c, c], axis=-1), jnp.concatenate([-s, s], axis=-1)


def _even_mixer(x, batch, seq_len, norm_g, w_in, short_w, short_b, f_w1, f_b1, f_w2, f_b2, f_w3, f_freq,
                hy_bias, q_g, k_g, w_out):
    d = x.shape[1]
    hy_d = hy_bias.shape[1]
    s0 = (HY_ORDER + 1) * hy_d
    nq = N_Q_HEADS * HEAD_DIM
    nk = N_KV_HEADS * HEAD_DIM
    g = norm_g.reshape(1, d)

    u_hy = _norm_proj(x, g, w_in[:, :s0].astype(BF16), F32, "hyena_in_proj")

    hid = LANES
    hpad = hid - f_w1.shape[1]
    zfeat = _position_features(seq_len, LANES)
    w1 = jnp.pad(f_w1, ((0, LANES - f_w1.shape[0]), (0, hpad)))
    b1 = jnp.pad(f_b1, (0, hpad)).reshape(1, hid)
    w2 = jnp.pad(f_w2, ((0, 0), (0, hpad), (0, hpad)))
    b2 = jnp.pad(f_b2, ((0, 0), (0, hpad))).reshape(HY_N_INNER, 1, hid)
    w3 = jnp.pad(f_w3, ((0, hpad), (0, 0)))
    freq = jnp.pad(f_freq, (0, hpad)).reshape(1, hid)
    max_decay = math.log(HY_DECAY_TARGET) / HY_FAST_DECAY
    min_decay = math.log(HY_DECAY_TARGET) / HY_SLOW_DECAY
    deltas = jnp.abs(jnp.linspace(min_decay, max_decay, hy_d, dtype=F32))
    deltas2 = jnp.tile(deltas, HY_ORDER).reshape(1, HY_ORDER * hy_d)
    ksum, kdiff = _hyena_filter_taps(zfeat, w1, b1, w2, b2, w3, freq, deltas2, seq_len, 256)
    p, pt = _dft_matrix(seq_len)
    ka, kc, ny = _hyena_spectrum(p, ksum, kdiff, 1024)
    bias = hy_bias.reshape(1, HY_ORDER * hy_d)
    sb = short_b.reshape(1, s0)
    z1 = _long_conv(u_hy, 0, u_hy, 1, short_w, sb, pt, ka, kc, ny, bias, 0, True, batch, seq_len, 256, F32)
    y_hy = _long_conv(z1, 0, u_hy, 2, short_w, sb, pt, ka, kc, ny, bias, 1, False, batch, seq_len, 256, BF16)

    w_qkv = jnp.concatenate([
        _deinterleave_heads(w_in[:, s0:s0 + nq], N_Q_HEADS),
        _deinterleave_heads(w_in[:, s0 + nq:s0 + nq + nk], N_KV_HEADS),
        w_in[:, s0 + nq + nk:]], axis=1).astype(BF16)
    qg = _deinterleave_heads(q_g, 1) * (HEAD_DIM ** -0.5 * LOG2E)
    kg = _deinterleave_heads(k_g, 1)
    head_gain = jnp.concatenate([jnp.tile(qg, N_Q_HEADS), jnp.tile(kg, N_KV_HEADS),
                                 jnp.ones((nk,), F32)]).reshape(1, nq + 2 * nk)
    cos, sin = _rope_tables(seq_len)
    qkv = _qkv_proj(x, g, w_qkv, head_gain, cos, sin, seq_len)
    y_at = _attention(qkv, batch, seq_len, min(1024, seq_len))

    return _out_proj(x, [y_hy, y_at], w_out.astype(BF16), "even_out_proj")


def _odd_mixer(x, batch, seq_len, norm_g, w_in, conv_w, conv_b, wa, ba, wx, bx, lam, w_out):
    d = x.shape[1]
    u = _norm_proj(x, norm_g.reshape(1, d), w_in.astype(BF16), F32, "rglru_in_proj")
    w_gates = (0.5 * jnp.concatenate([wa, wx], axis=-1)).astype(BF16)
    b_gates = 0.5 * jnp.concatenate([ba, bx], axis=-1)[:, :, None, :]
    y = _rglru(u, conv_w, conv_b, w_gates, b_gates, lam, batch, seq_len)
    return _out_proj(x, [y], w_out.astype(BF16), "rglru_out_proj")


def kernel(x_prompt, x_sample, ffn_norm, ffn_w_gate, ffn_w_up, ffn_w_down, even_norm, even_w_in, hy_short_w, hy_short_b, hy_filt_w1, hy_filt_b1, hy_filt_w2, hy_filt_b2, hy_filt_w3, hy_filt_freq, hy_bias, q_norm, k_norm, even_w_out, odd_norm, odd_w_in, rg_conv_w, rg_conv_b, rg_wa, rg_ba, rg_wx, rg_bx, rg_lambda, odd_w_out):
    bp, seq_len, d = x_prompt.shape
    bs = x_sample.shape[0]
    assert x_sample.shape[1:] == (seq_len, d)
    batch = bp + bs
    rows_p = bp * seq_len

    depth = ffn_norm.shape[0]
    ff = ffn_w_gate.shape[-1]
    fp = -(-ff // TF) * TF
    wg = _cast_pad(ffn_w_gate.reshape(depth * 2, d, ff), d, fp, 1024, TF)
    wu = _cast_pad(ffn_w_up.reshape(depth * 2, d, ff), d, fp, 1024, TF)
    wd = _cast_pad(ffn_w_down.reshape(depth * 2, ff, d), fp, d, TF, d)
    fg = ffn_norm.reshape(depth * 2, 1, d)
    rows_s = bs * seq_len
    total = rows_p + rows_s

    x = None
    for layer in range(depth):
        if layer == 0:
            x = _ffn_half(x_prompt.reshape(rows_p, d), fg, 0, wg, wu, wd, out_rows=total)
            x = _ffn_half(x_sample.reshape(rows_s, d), fg, 0, wg, wu, wd, out_rows=total, out_row0=rows_p, dest=x)
        else:
            x = _ffn_half(x, fg, 2 * layer, wg, wu, wd)
        j = layer // 2
        if layer % 2 == 0:
            x = _even_mixer(x, batch, seq_len, even_norm[j], even_w_in[j], hy_short_w[j], hy_short_b[j],
                            hy_filt_w1[j], hy_filt_b1[j], hy_filt_w2[j], hy_filt_b2[j], hy_filt_w3[j],
                            hy_filt_freq[j], hy_bias[j], q_norm[j], k_norm[j], even_w_out[j])
        else:
            x = _odd_mixer(x, batch, seq_len, odd_norm[j], odd_w_in[j], rg_conv_w[j], rg_conv_b[j],
                           rg_wa[j], rg_ba[j], rg_wx[j], rg_bx[j], rg_lambda[j], odd_w_out[j])
        if layer == depth - 1:
            y_prompt = _ffn_half(x, fg, 2 * layer + 1, wg, wu, wd, rows=rows_p)
            y_sample = _ffn_half(x, fg, 2 * layer + 1, wg, wu, wd, rows=rows_s, x_row0=rows_p)
        else:
            x = _ffn_half(x, fg, 2 * layer + 1, wg, wu, wd)

    return (y_prompt.reshape(bp, seq_len, d), y_sample.reshape(bs, seq_len, d))
```

```python
import functools
import math

import jax
import jax.numpy as jnp
from jax import lax
from jax.experimental import pallas as pl
from jax.experimental.pallas import tpu as pltpu

F32 = jnp.float32
BF16 = jnp.bfloat16

NORM_EPS = 1e-6
GRID_W = 64
HEAD_DIM = 128
N_Q_HEADS = 8
N_KV_HEADS = 2
GQA_GROUP = N_Q_HEADS // N_KV_HEADS
ROPE_THETA = 10000.0
HY_ORDER = 2
HY_SHORT_W = 3
HY_EMB_DIM = 33
HY_BANDS = (HY_EMB_DIM - 1) // 2
HY_N_INNER = 2
HY_FAST_DECAY = 0.3
HY_SLOW_DECAY = 1.5
HY_DECAY_TARGET = 1e-2
RNN_HEADS = 8
RNN_CONV_W = 4
RNN_C = 8.0
LOG2E = 1.4426950408889634

LANES = 128
SUBLANES = 8
VMEM_LIMIT = 58 * 1024 * 1024

TM = 512
TM_FFN = 1024
TF = 512
MAX_CAST_SLICES = 16
TN = 1024
ATT_KC = 256
FB = 256
CONV_ROWS = 64
GATE_ROWS = 512
SCAN_GROUP = 8
SCAN_PITCH_PAD = 8


def _params(sem, vmem=VMEM_LIMIT):
    return pltpu.CompilerParams(dimension_semantics=sem, vmem_limit_bytes=vmem)


def _resident(shape):
    return pl.BlockSpec(shape, lambda *_: (0,) * len(shape), pipeline_mode=pl.Buffered(1))


def _rms(x, g):
    ms = jnp.mean(x * x, axis=-1, keepdims=True)
    return x * lax.rsqrt(ms + NORM_EPS) * g


def _cast_pad_kernel(w_ref, o_ref, *, rows, cols):
    br, bc = o_ref.shape
    r = lax.broadcasted_iota(jnp.int32, (br, bc), 0) + pl.program_id(1) * br
    c = lax.broadcasted_iota(jnp.int32, (br, bc), 1) + pl.program_id(2) * bc
    w = jnp.where(r < rows, jnp.where(c < cols, w_ref[...], 0.0), 0.0)
    o_ref[...] = w.astype(o_ref.dtype)


def _cast_pad(w, count, rows_out, cols_out, br, bc):
    _, rows, cols = w.shape
    n = count
    return pl.pallas_call(
        functools.partial(_cast_pad_kernel, rows=rows, cols=cols),
        grid=(n, rows_out // br, cols_out // bc),
        in_specs=[pl.BlockSpec((None, br, bc), lambda k, i, j: (k, i, j))],
        out_specs=pl.BlockSpec((None, br, bc), lambda k, i, j: (k, i, j)),
        out_shape=jax.ShapeDtypeStruct((n, rows_out, cols_out), BF16),
        compiler_params=_params(("parallel", "parallel", "parallel")),
        name="cast_pad",
    )(w)


def _ffn_kernel(x_ref, g_ref, wg_ref, wu_ref, wd_ref, *rest, has_dest, has_side, ff):
    rest = rest[1:] if has_dest else rest
    j = pl.program_id(1)
    if has_side:
        sg_ref, su_ref, sd_ref, o_ref, cg_ref, cu_ref, cd_ref, n_ref = rest
    else:
        o_ref, n_ref = rest

    def cast_next_slices():
        if has_side:
            keep_col = lax.broadcasted_iota(jnp.int32, sg_ref.shape, 1) + j * TF < ff
            cg_ref[...] = jnp.where(keep_col, sg_ref[...], 0.0).astype(BF16)
            cu_ref[...] = jnp.where(keep_col, su_ref[...], 0.0).astype(BF16)
            keep_row = lax.broadcasted_iota(jnp.int32, sd_ref.shape, 0) + j * TF < ff
            cd_ref[...] = jnp.where(keep_row, sd_ref[...], 0.0).astype(BF16)

    def contribution():
        cast_next_slices()
        n = n_ref[...]
        hg = jnp.dot(n, wg_ref[...], preferred_element_type=F32)
        hu = jnp.dot(n, wu_ref[...], preferred_element_type=F32)
        h = ((hg * jax.nn.sigmoid(hg)) * (hu * 0.5)).astype(BF16)
        return jnp.dot(h, wd_ref[...], preferred_element_type=F32)

    @pl.when(j == 0)
    def _():
        n_ref[...] = _rms(x_ref[...], g_ref[...]).astype(BF16)
        o_ref[...] = x_ref[...] + contribution()

    @pl.when(j > 0)
    def _():
        o_ref[...] += contribution()


def _ffn_half(x, g, g_idx, weights, *, rows=None, x_row0=0, out_rows=None, out_row0=0, dest=None, cast_next=None):
    wg, wu, wd = weights
    d = x.shape[1]
    fp = wg.shape[2]
    rows = x.shape[0] if rows is None else rows
    out_rows = rows if out_rows is None else out_rows
    xb = x_row0 // TM_FFN
    ob = out_row0 // TM_FFN
    in_specs = [
        pl.BlockSpec((TM_FFN, d), lambda i, j: (i + xb, 0)),
        pl.BlockSpec((None, 1, d), lambda i, j: (g_idx, 0, 0)),
        pl.BlockSpec((None, d, TF), lambda i, j: (0, 0, j)),
        pl.BlockSpec((None, d, TF), lambda i, j: (0, 0, j)),
        pl.BlockSpec((None, TF, d), lambda i, j: (0, j, 0)),
    ]
    args = [x, g, wg, wu, wd]
    out_specs = [pl.BlockSpec((TM_FFN, d), lambda i, j: (i + ob, 0))]
    out_shape = [jax.ShapeDtypeStruct((out_rows, d), F32)]
    aliases = {}
    ff = 0
    if dest is not None:
        in_specs.append(pl.BlockSpec(memory_space=pl.ANY))
        args.append(dest)
        aliases = {len(args) - 1: 0}
    if cast_next is not None:
        gate32, up32, down32, k = cast_next
        ff = gate32.shape[2]
        n_side = max(s for s in (1, 2, 4, 8, MAX_CAST_SLICES) if s <= rows // TM_FFN and d % (s * LANES) == 0)
        cs = d // n_side

        def side(i):
            return jnp.minimum(i, n_side - 1)

        in_specs += [
            pl.BlockSpec((None, cs, TF), lambda i, j: (k, side(i), j)),
            pl.BlockSpec((None, cs, TF), lambda i, j: (k, side(i), j)),
            pl.BlockSpec((None, TF, cs), lambda i, j: (k, j, side(i))),
        ]
        args += [gate32, up32, down32]
        out_specs += [
            pl.BlockSpec((None, cs, TF), lambda i, j: (0, side(i), j)),
            pl.BlockSpec((None, cs, TF), lambda i, j: (0, side(i), j)),
            pl.BlockSpec((None, TF, cs), lambda i, j: (0, j, side(i))),
        ]
        out_shape += [jax.ShapeDtypeStruct((1, d, fp), BF16), jax.ShapeDtypeStruct((1, d, fp), BF16),
                      jax.ShapeDtypeStruct((1, fp, d), BF16)]
    res = pl.pallas_call(
        functools.partial(_ffn_kernel, has_dest=dest is not None, has_side=cast_next is not None, ff=ff),
        grid=(rows // TM_FFN, fp // TF),
        in_specs=in_specs,
        out_specs=out_specs,
        out_shape=out_shape,
        scratch_shapes=[pltpu.VMEM((TM_FFN, d), BF16)],
        input_output_aliases=aliases,
        compiler_params=_params(("arbitrary", "arbitrary")),
        name="ffn_half",
    )(*args)
    return (res[0], tuple(res[1:])) if cast_next is not None else res[0]


def _proj_kernel(x_ref, g_ref, w_ref, o_ref, *, gelu_cols):
    n = _rms(x_ref[...], g_ref[...]).astype(BF16)
    for c in range(0, w_ref.shape[1], TN):
        u = jnp.dot(n, w_ref[:, c:c + TN], preferred_element_type=F32)
        if c < gelu_cols:
            u = _gelu_tanh(u)
        o_ref[:, c:c + TN] = u.astype(o_ref.dtype)


def _norm_proj(x, g, w, out_dtype, name, gelu_cols=0):
    t, d = x.shape
    n = w.shape[1]
    assert gelu_cols % TN == 0
    return pl.pallas_call(
        functools.partial(_proj_kernel, gelu_cols=gelu_cols),
        grid=(t // TM,),
        in_specs=[
            pl.BlockSpec((TM, d), lambda i: (i, 0)),
            _resident((1, d)),
            _resident((d, n)),
        ],
        out_specs=pl.BlockSpec((TM, n), lambda i: (i, 0)),
        out_shape=jax.ShapeDtypeStruct((t, n), out_dtype),
        compiler_params=_params(("parallel",)),
        name=name,
    )(x, g, w)


def _qkv_kernel(x_ref, g_ref, w_ref, hg_ref, cos_ref, sin_ref, o_ref, *, n_rope):
    n = _rms(x_ref[...], g_ref[...]).astype(BF16)
    u = jnp.dot(n, w_ref[...], preferred_element_type=F32)
    c = cos_ref[...]
    s = sin_ref[...]
    for h in range(u.shape[1] // HEAD_DIM):
        sl = slice(h * HEAD_DIM, (h + 1) * HEAD_DIM)
        uh = u[:, sl]
        if h < n_rope:
            ms = jnp.mean(uh * uh, axis=-1, keepdims=True)
            uh = uh * lax.rsqrt(ms + NORM_EPS) * hg_ref[:, sl]
            uh = uh * c + pltpu.roll(uh, HEAD_DIM // 2, 1) * s
        o_ref[:, sl] = uh.astype(o_ref.dtype)


def _qkv_proj(x, g, w, head_gain, cos, sin, seq_len):
    t, d = x.shape
    n = w.shape[1]
    tiles_per_seq = seq_len // TM
    n_rope = N_Q_HEADS + N_KV_HEADS
    return pl.pallas_call(
        functools.partial(_qkv_kernel, n_rope=n_rope),
        grid=(t // TM,),
        in_specs=[
            pl.BlockSpec((TM, d), lambda i: (i, 0)),
            _resident((1, d)),
            _resident((d, n)),
            _resident((1, n)),
            pl.BlockSpec((TM, HEAD_DIM), lambda i: (i % tiles_per_seq, 0)),
            pl.BlockSpec((TM, HEAD_DIM), lambda i: (i % tiles_per_seq, 0)),
        ],
        out_specs=pl.BlockSpec((TM, n), lambda i: (i, 0)),
        out_shape=jax.ShapeDtypeStruct((t, n), BF16),
        compiler_params=_params(("parallel",)),
        name="qkv_proj",
    )(x, g, w, head_gain, cos, sin)


def _outproj_kernel(*refs, n_y):
    r_ref = refs[0]
    y_refs = refs[1:1 + n_y]
    w_ref = refs[1 + n_y]
    o_ref = refs[2 + n_y]
    ys = [y_ref[...] for y_ref in y_refs]
    for c in range(0, o_ref.shape[1], TN):
        acc = r_ref[:, c:c + TN]
        k0 = 0
        for y in ys:
            k = y.shape[1]
            acc = acc + jnp.dot(y, w_ref[k0:k0 + k, c:c + TN], preferred_element_type=F32)
            k0 += k
        o_ref[:, c:c + TN] = acc


def _out_proj(res, ys, w, name):
    t, d = res.shape
    return pl.pallas_call(
        functools.partial(_outproj_kernel, n_y=len(ys)),
        grid=(t // TM,),
        in_specs=[pl.BlockSpec((TM, d), lambda i: (i, 0))]
        + [pl.BlockSpec((TM, y.shape[1]), lambda i: (i, 0)) for y in ys]
        + [_resident(w.shape)],
        out_specs=pl.BlockSpec((TM, d), lambda i: (i, 0)),
        out_shape=jax.ShapeDtypeStruct((t, d), F32),
        compiler_params=_params(("parallel",)),
        name=name,
    )(res, *ys, w)


def _attn_kernel(q_ref, k_ref, v_ref, o_ref, st_ref):
    seq = k_ref.shape[0]
    n_heads = q_ref.shape[1] // HEAD_DIM
    nt = (((1,), (1,)), ((), ()))
    tn = (((0,), (0,)), ((), ()))
    chunks = [slice(c * ATT_KC, (c + 1) * ATT_KC) for c in range(seq // ATT_KC)]

    def scores(g, rows, slot, m):
        s = lax.dot_general(k_ref[rows, :], q_ref[:, g * HEAD_DIM:(g + 1) * HEAD_DIM], nt,
                            preferred_element_type=F32)
        st_ref[slot, rows, :] = s
        cm = jnp.max(s, axis=0, keepdims=True)
        return cm if m is None else jnp.maximum(m, cm)

    m = None
    for rows in chunks:
        m = scores(0, rows, 0, m)
    for g in range(n_heads):
        cur = g % 2
        acc = l = m_next = None
        for rows in chunks:
            if g + 1 < n_heads:
                m_next = scores(g + 1, rows, 1 - cur, m_next)
            p = jnp.exp2(st_ref[cur, rows, :] - m)
            ps = jnp.sum(p, axis=0, keepdims=True)
            pv = lax.dot_general(v_ref[rows, :], p.astype(BF16), tn, preferred_element_type=F32)
            l = ps if l is None else l + ps
            acc = pv if acc is None else acc + pv
        o_ref[:, g * HEAD_DIM:(g + 1) * HEAD_DIM] = (acc / l).T.astype(o_ref.dtype)
        m = m_next


def _attention(qkv, batch, seq_len, tq):
    t = qkv.shape[0]
    gw = GQA_GROUP * HEAD_DIM
    nq = seq_len // tq
    k_off = N_Q_HEADS
    v_off = N_Q_HEADS + N_KV_HEADS
    return pl.pallas_call(
        _attn_kernel,
        grid=(batch, N_KV_HEADS, nq),
        in_specs=[
            pl.BlockSpec((tq, gw), lambda b, h, i: (b * nq + i, h)),
            pl.BlockSpec((seq_len, HEAD_DIM), lambda b, h, i: (b, k_off + h)),
            pl.BlockSpec((seq_len, HEAD_DIM), lambda b, h, i: (b, v_off + h)),
        ],
        out_specs=pl.BlockSpec((tq, gw), lambda b, h, i: (b * nq + i, h)),
        out_shape=jax.ShapeDtypeStruct((t, N_Q_HEADS * HEAD_DIM), BF16),
        scratch_shapes=[pltpu.VMEM((2, seq_len, tq), F32)],
        compiler_params=_params(("parallel", "parallel", "arbitrary")),
        name="attention",
    )(qkv, qkv, qkv)


def _fill_padded(src_ref, pad_ref):
    seq = src_ref.shape[0]
    zeros = jnp.zeros((SUBLANES, pad_ref.shape[1]), F32)
    pad_ref[0:SUBLANES, :] = zeros
    pad_ref[seq + SUBLANES:seq + 2 * SUBLANES, :] = zeros
    pad_ref[SUBLANES:seq + SUBLANES, :] = src_ref[...]


def _short_conv_rows(pad_ref, w, b, r0):
    n = CONV_ROWS + 2 * SUBLANES
    win = pad_ref[pl.ds(r0, n), :]
    taps = (pltpu.roll(win, 1, 0), win, pltpu.roll(win, n - 1, 0))
    y = b
    for j in range(HY_SHORT_W):
        y = y + taps[j][SUBLANES:SUBLANES + CONV_ROWS, :] * w[j:j + 1, :]
    return y


def _filter_kernel(z_ref, w1_ref, b1_ref, w2_ref, b2_ref, w3_ref, fr_ref, dl_ref, ks_ref, kd_ref, *, seq_len):
    hi = lax.Precision.HIGHEST
    fr = fr_ref[...]
    h = jnp.sin(fr * (jnp.dot(z_ref[...], w1_ref[...], precision=hi, preferred_element_type=F32) + b1_ref[...]))
    for i in range(HY_N_INNER):
        h = jnp.sin(fr * (jnp.dot(h, w2_ref[i], precision=hi, preferred_element_type=F32) + b2_ref[i]))
    k = jnp.dot(h, w3_ref[...], precision=hi, preferred_element_type=F32)
    tl, half = ks_ref.shape
    pos = lax.broadcasted_iota(jnp.int32, (tl, half), 0) + pl.program_id(0) * tl
    tnorm = pos.astype(F32) / (seq_len - 1)
    decay = jnp.exp(-tnorm * dl_ref[...])
    kf = k[:, :half] * decay
    kb = jnp.where(pos == 0, 0.0, k[:, half:] * decay)
    ks_ref[...] = (kf + kb).astype(ks_ref.dtype)
    kd_ref[...] = (kf - kb).astype(kd_ref.dtype)


def _hyena_filter_taps(zfeat, w1, b1, w2, b2, w3, freq, deltas2, seq_len, tl):
    half = deltas2.shape[1]
    return pl.pallas_call(
        functools.partial(_filter_kernel, seq_len=seq_len),
        grid=(seq_len // tl,),
        in_specs=[
            pl.BlockSpec((tl, zfeat.shape[1]), lambda i: (i, 0)),
            _resident(w1.shape),
            _resident(b1.shape),
            _resident(w2.shape),
            _resident(b2.shape),
            _resident(w3.shape),
            _resident(freq.shape),
            _resident(deltas2.shape),
        ],
        out_specs=[pl.BlockSpec((tl, half), lambda i: (i, 0))] * 2,
        out_shape=[jax.ShapeDtypeStruct((seq_len, half), BF16)] * 2,
        compiler_params=_params(("parallel",)),
        name="hyena_filter_taps",
    )(zfeat, w1, b1, w2, b2, w3, freq, deltas2)


def _spectrum_kernel(p_ref, ks_ref, kd_ref, ka_ref, kc_ref, ny_ref, *, period):
    fb = ka_ref.shape[0]
    ks = ks_ref[...]
    re = jnp.dot(p_ref[0:fb, :], ks, preferred_element_type=F32)
    im = jnp.dot(p_ref[fb:2 * fb, :], kd_ref[...], preferred_element_type=F32)
    ny = jnp.dot(p_ref[fb:fb + SUBLANES, :], ks, preferred_element_type=F32)
    is_dc = (lax.broadcasted_iota(jnp.int32, re.shape, 0) + pl.program_id(1) * fb) == 0
    ka_ref[...] = re * jnp.where(is_dc, 1.0 / period, 2.0 / period)
    kc_ref[...] = jnp.where(is_dc, 0.0, im * (2.0 / period))
    ny_ref[...] = ny * (1.0 / period)


def _hyena_spectrum(p, ksum, kdiff, tcs):
    seq_len, cols = ksum.shape
    nf = p.shape[0] // (2 * FB)
    return pl.pallas_call(
        functools.partial(_spectrum_kernel, period=2 * seq_len),
        grid=(cols // tcs, nf),
        in_specs=[
            pl.BlockSpec((2 * FB, seq_len), lambda c, f: (f, 0)),
            pl.BlockSpec((seq_len, tcs), lambda c, f: (0, c)),
            pl.BlockSpec((seq_len, tcs), lambda c, f: (0, c)),
        ],
        out_specs=[
            pl.BlockSpec((FB, tcs), lambda c, f: (f, c)),
            pl.BlockSpec((FB, tcs), lambda c, f: (f, c)),
            pl.BlockSpec((SUBLANES, tcs), lambda c, f: (f, c)),
        ],
        out_shape=[
            jax.ShapeDtypeStruct((nf * FB, cols), F32),
            jax.ShapeDtypeStruct((nf * FB, cols), F32),
            jax.ShapeDtypeStruct((nf * SUBLANES, cols), F32),
        ],
        compiler_params=_params(("parallel", "parallel")),
        name="hyena_filter_spectrum",
    )(p, ksum, kdiff)


def _longconv_kernel(z_ref, gate_ref, swz_ref, sbz_ref, swg_ref, sbg_ref, pt_ref, ka_ref, kc_ref, ny_ref,
                     bias_ref, o_ref, pad_ref, zc_ref, zb_ref, y_ref, *, conv_z):
    seq = z_ref.shape[0]
    nf = seq // FB

    if conv_z:
        _fill_padded(z_ref, pad_ref)
        w = swz_ref[...]
        b = sbz_ref[...]

        def body(i, carry):
            r0 = pl.multiple_of(i * CONV_ROWS, CONV_ROWS)
            y = _short_conv_rows(pad_ref, w, b, r0)
            zc_ref[pl.ds(r0, CONV_ROWS), :] = y
            zb_ref[pl.ds(r0, CONV_ROWS), :] = y.astype(BF16)
            return carry

        lax.fori_loop(0, seq // CONV_ROWS, body, 0)
    else:
        zb_ref[...] = z_ref[...].astype(BF16)
    _fill_padded(gate_ref, pad_ref)

    tn = (((0,), (0,)), ((), ()))
    for f in range(nf):
        zb = zb_ref[...]
        xre = lax.dot_general(pt_ref[:, f * FB:(f + 1) * FB], zb, tn, preferred_element_type=F32)
        xim = lax.dot_general(pt_ref[:, seq + f * FB:seq + (f + 1) * FB], zb, tn, preferred_element_type=F32)
        ka = ka_ref[f * FB:(f + 1) * FB, :]
        kc = kc_ref[f * FB:(f + 1) * FB, :]
        kd = ka
        if f == 0:
            is_dc = lax.broadcasted_iota(jnp.int32, ka.shape, 0) == 0
            kd = jnp.where(is_dc, ny_ref[0:1, :], ka)
        y_ref[f * FB:(f + 1) * FB, :] = (xre * ka - xim * kc).astype(BF16)
        y_ref[seq + f * FB:seq + (f + 1) * FB, :] = (xre * kc + xim * kd).astype(BF16)

    wg = swg_ref[...]
    bg = sbg_ref[...]
    bias = bias_ref[...]
    zsrc = zc_ref if conv_z else z_ref
    for r in range(nf):
        conv = jnp.dot(pt_ref[r * FB:(r + 1) * FB, :], y_ref[...], preferred_element_type=F32)
        for k in range(FB // CONV_ROWS):
            r0 = r * FB + k * CONV_ROWS
            g = _short_conv_rows(pad_ref, wg, bg, r0)
            y = conv[k * CONV_ROWS:(k + 1) * CONV_ROWS, :]
            o_ref[r0:r0 + CONV_ROWS, :] = (g * (y + zsrc[r0:r0 + CONV_ROWS, :] * bias)).astype(o_ref.dtype)


def _long_conv(z, z_col, gate, gate_col, short_w, short_b, pt, ka, kc, ny, bias, order, conv_z,
               batch, seq_len, tc, out_dtype):
    c = bias.shape[1] // HY_ORDER
    t = z.shape[0]
    nc = c // tc
    zc_col = z_col if conv_z else 0
    return pl.pallas_call(
        functools.partial(_longconv_kernel, conv_z=conv_z),
        grid=(nc, batch),
        in_specs=[
            pl.BlockSpec((seq_len, tc), lambda j, b: (b, z_col * nc + j)),
            pl.BlockSpec((seq_len, tc), lambda j, b: (b, gate_col * nc + j)),
            pl.BlockSpec((HY_SHORT_W, tc), lambda j, b: (0, zc_col * nc + j)),
            pl.BlockSpec((1, tc), lambda j, b: (0, zc_col * nc + j)),
            pl.BlockSpec((HY_SHORT_W, tc), lambda j, b: (0, gate_col * nc + j)),
            pl.BlockSpec((1, tc), lambda j, b: (0, gate_col * nc + j)),
            _resident(pt.shape),
            pl.BlockSpec((seq_len, tc), lambda j, b: (0, order * nc + j)),
            pl.BlockSpec((seq_len, tc), lambda j, b: (0, order * nc + j)),
            pl.BlockSpec((SUBLANES, tc), lambda j, b: (0, order * nc + j)),
            pl.BlockSpec((1, tc), lambda j, b: (0, order * nc + j)),
        ],
        out_specs=pl.BlockSpec((seq_len, tc), lambda j, b: (b, j)),
        out_shape=jax.ShapeDtypeStruct((t, c), out_dtype),
        scratch_shapes=[
            pltpu.VMEM((seq_len + 2 * SUBLANES, tc), F32),
            pltpu.VMEM((seq_len, tc) if conv_z else (SUBLANES, LANES), F32),
            pltpu.VMEM((seq_len, tc), BF16),
            pltpu.VMEM((2 * seq_len, tc), BF16),
        ],
        compiler_params=_params(("parallel", "parallel")),
        name="hyena_long_conv",
    )(z, gate, short_w, short_b, short_w, short_b, pt, ka, kc, ny, bias)


def _dft_matrix(seq_len):
    period = 2 * seq_len
    split = 64
    f = jnp.arange(seq_len, dtype=jnp.int32)[:, None]

    def trig(tvals):
        ang = ((f * tvals[None, :]) % period).astype(F32) * (2.0 * math.pi / period)
        return jnp.cos(ang), jnp.sin(ang)

    ca, sa = trig(jnp.arange(seq_len // split, dtype=jnp.int32) * split)
    cb, sb = trig(jnp.arange(split, dtype=jnp.int32))
    re = (ca[:, :, None] * cb[:, None, :] - sa[:, :, None] * sb[:, None, :]).reshape(seq_len, seq_len)
    im = -(sa[:, :, None] * cb[:, None, :] + ca[:, :, None] * sb[:, None, :]).reshape(seq_len, seq_len)
    nyq = jnp.where(jnp.arange(seq_len) % 2 == 0, 1.0, -1.0).astype(F32)
    im = im.at[0].set(nyq)
    nf = seq_len // FB
    p = jnp.stack([re.reshape(nf, FB, seq_len), im.reshape(nf, FB, seq_len)], axis=1)
    p = p.reshape(2 * seq_len, seq_len).astype(BF16)
    ret = (ca.T[:, None, :] * cb.T[None, :, :] - sa.T[:, None, :] * sb.T[None, :, :]).reshape(seq_len, seq_len)
    imt = -(sa.T[:, None, :] * cb.T[None, :, :] + ca.T[:, None, :] * sb.T[None, :, :]).reshape(seq_len, seq_len)
    imt = imt.at[:, 0].set(nyq)
    return p, jnp.concatenate([ret, imt], axis=1).astype(BF16)


def _position_features(seq_len, width):
    pos = jnp.arange(seq_len, dtype=F32)
    t = pos / (seq_len - 1)
    w = 2.0 * math.pi * pos / seq_len
    f = jnp.linspace(1e-4, HY_BANDS - 1, HY_BANDS, dtype=F32)
    fw = w[:, None] * f[None, :]
    z = jnp.concatenate([t[:, None], jnp.cos(fw), -jnp.sin(fw)], axis=-1)
    return jnp.pad(z, ((0, 0), (0, width - z.shape[1])))


def _softplus(x):
    return jnp.maximum(x, 0.0) + jnp.log1p(jnp.exp(-jnp.abs(x)))


def _gelu_tanh(x):
    return 0.5 * x * (1.0 + jnp.tanh(math.sqrt(2.0 / math.pi) * (x + 0.044715 * (x * x * x))))


def _rglru_kernel(g_ref, x_ref, cw_ref, cb_ref, w_ref, bias_ref, lam_ref, o_ref,
                  xp_ref, xq_ref, a_ref, b_ref, hl_ref, al_ref, hs_ref):
    seq, width = x_ref.shape
    nck = SUBLANES
    chunk = seq // nck
    pitch = chunk + SCAN_PITCH_PAD
    slabs = width // LANES
    halo = RNN_CONV_W - 1

    for c in range(nck):
        for sl in range(slabs):
            xp_ref[sl, c * pitch:c * pitch + chunk, :] = x_ref[c * chunk:(c + 1) * chunk, sl * LANES:(sl + 1) * LANES]

    def to_chunk_major(i, carry):
        s0 = i * SCAN_GROUP
        row0 = pl.multiple_of((s0 + halo) * nck, nck)
        for k in range(SCAN_GROUP):
            for sl in range(slabs):
                xq_ref[pl.ds(row0 + k * nck, nck), sl * LANES:(sl + 1) * LANES] = (
                    xp_ref[sl, pl.ds(s0 + k, nck, stride=pitch), :])
        return carry

    lax.fori_loop(0, chunk // SCAN_GROUP, to_chunk_major, 0)

    sub = lax.broadcasted_iota(jnp.int32, (nck, width), 0)
    for k in range(halo):
        tail = xq_ref[(chunk + k) * nck:(chunk + k + 1) * nck, :]
        xq_ref[k * nck:(k + 1) * nck, :] = jnp.where(sub == 0, 0.0, pltpu.roll(tail, 1, 0))
        head = xq_ref[(halo + k) * nck:(halo + k + 1) * nck, :]
        xq_ref[(halo + chunk + k) * nck:(halo + chunk + k + 1) * nck, :] = jnp.where(
            sub == nck - 1, 0.0, pltpu.roll(head, nck - 1, 0))

    for d in range(2):
        ch = (0.5 * RNN_C) * _softplus(-lam_ref[d:d + 1, :])
        reset_row = 0 if d == 0 else seq - 1

        def gates(bi, carry):
            base = pl.multiple_of(bi * GATE_ROWS, GATE_ROWS)
            xc = cb_ref[d:d + 1, :]
            for j in range(RNN_CONV_W):
                step = j if d == 0 else 2 * halo - j
                xc = xc + xq_ref[pl.ds(base + step * nck, GATE_ROWS), :] * cw_ref[d, j:j + 1, :]
            gt = jnp.dot(xc.astype(BF16), w_ref[d, 0], preferred_element_type=F32) + bias_ref[d, 0]
            tr = jnp.tanh(gt[:, :width])
            ti = jnp.tanh(gt[:, width:])
            nla = tr * ch + ch
            a = jnp.exp2(nla * (-LOG2E))
            m2 = jnp.tanh(nla) * (a * a + 1.0)
            mult = jnp.where(m2 > 0.0, m2 * lax.rsqrt(m2), 0.0)
            ixc = (0.5 * ti + 0.5) * xc
            bt = mult * ixc
            a_ref[d, pl.ds(base, GATE_ROWS), :] = a
            b_ref[d, pl.ds(base, GATE_ROWS), :] = bt

            @pl.when(bi == reset_row // GATE_ROWS)
            def _():
                lo = (reset_row % GATE_ROWS) // nck * nck
                fixed = jnp.where(sub == reset_row % nck, ixc[lo:lo + nck, :], bt[lo:lo + nck, :])
                b_ref[d, pl.ds(base + lo, nck), :] = fixed

            return carry

        lax.fori_loop(0, seq // GATE_ROWS, gates, 0)

    def scan(i, carry):
        state = list(carry)
        group_rows = SCAN_GROUP * nck
        for k in range(SCAN_GROUP):
            for d in range(2):
                h, acc = state[2 * d], state[2 * d + 1]
                if d == 0:
                    row = pl.multiple_of(i * group_rows, nck) + k * nck
                else:
                    row = pl.multiple_of((chunk - SCAN_GROUP) * nck - i * group_rows, nck) + (SCAN_GROUP - 1 - k) * nck
                av = a_ref[d, pl.ds(row, nck), :]
                h = av * h + b_ref[d, pl.ds(row, nck), :]
                acc = acc * av
                hl_ref[d, pl.ds(row, nck), :] = h
                al_ref[d, pl.ds(row, nck), :] = acc
                state[2 * d], state[2 * d + 1] = h, acc
        return tuple(state)

    init = (jnp.zeros((nck, width), F32), jnp.ones((nck, width), F32)) * 2
    ends = lax.fori_loop(0, chunk // SCAN_GROUP, scan, init)

    carries = []
    for d in range(2):
        h_end, a_end = ends[2 * d], ends[2 * d + 1]
        rows = [None] * nck
        cur = jnp.zeros((1, width), F32)
        for c in (range(nck) if d == 0 else range(nck - 1, -1, -1)):
            rows[c] = cur
            cur = a_end[c:c + 1, :] * cur + h_end[c:c + 1, :]
        carries.append(jnp.concatenate(rows, axis=0))

    def fold(i, carry):
        s0 = i * SCAN_GROUP
        row0 = pl.multiple_of(s0 * nck, nck)
        for k in range(SCAN_GROUP):
            rows = pl.ds(row0 + k * nck, nck)
            hv = (hl_ref[0, rows, :] + al_ref[0, rows, :] * carries[0]
                  + hl_ref[1, rows, :] + al_ref[1, rows, :] * carries[1])
            for sl in range(slabs):
                hs_ref[sl, pl.ds(s0 + k, nck, stride=pitch), :] = hv[:, sl * LANES:(sl + 1) * LANES]
        return carry

    lax.fori_loop(0, chunk // SCAN_GROUP, fold, 0)

    def finish(c, carry):
        r0 = pl.multiple_of(c * chunk, chunk)
        p0 = pl.multiple_of(c * pitch, SUBLANES)
        gate = g_ref[pl.ds(r0, chunk), :]
        for sl in range(slabs):
            o_ref[pl.ds(r0, chunk), sl * LANES:(sl + 1) * LANES] = (
                hs_ref[sl, pl.ds(p0, chunk), :] * gate[:, sl * LANES:(sl + 1) * LANES]).astype(o_ref.dtype)
        return carry

    lax.fori_loop(0, nck, finish, 0)


def _rglru(u, conv_w, conv_b, w_gates, b_gates, lam, batch, seq_len):
    t, two_d = u.shape
    d = two_d // 2
    width = d // RNN_HEADS
    slabs = width // LANES
    chunk = seq_len // SUBLANES
    pitched = SUBLANES * (chunk + SCAN_PITCH_PAD)
    halo_rows = 2 * (RNN_CONV_W - 1) * SUBLANES
    return pl.pallas_call(
        _rglru_kernel,
        grid=(batch, RNN_HEADS),
        in_specs=[
            pl.BlockSpec((seq_len, width), lambda b, h: (b, h)),
            pl.BlockSpec((seq_len, width), lambda b, h: (b, RNN_HEADS + h)),
            pl.BlockSpec((2, RNN_CONV_W, width), lambda b, h: (0, 0, h)),
            pl.BlockSpec((2, width), lambda b, h: (0, h)),
            pl.BlockSpec((2, 1, width, 2 * width), lambda b, h: (0, h, 0, 0)),
            pl.BlockSpec((2, 1, 1, 2 * width), lambda b, h: (0, h, 0, 0)),
            pl.BlockSpec((2, width), lambda b, h: (0, h)),
        ],
        out_specs=pl.BlockSpec((seq_len, width), lambda b, h: (b, h)),
        out_shape=jax.ShapeDtypeStruct((t, d), BF16),
        scratch_shapes=[
            pltpu.VMEM((slabs, pitched, LANES), F32),
            pltpu.VMEM((seq_len + halo_rows, width), F32),
            pltpu.VMEM((2, seq_len, width), F32),
            pltpu.VMEM((2, seq_len, width), F32),
            pltpu.VMEM((2, seq_len, width), F32),
            pltpu.VMEM((2, seq_len, width), F32),
            pltpu.VMEM((slabs, pitched, LANES), F32),
        ],
        compiler_params=_params(("parallel", "parallel")),
        name="rglru",
    )(u, u, conv_w, conv_b, w_gates, b_gates, lam)


def _deinterleave_heads(w, n_heads):
    lead = w.shape[:-1]
    w = w.reshape(lead + (n_heads, HEAD_DIM // 2, 2))
    w = jnp.swapaxes(w, -1, -2)
    return w.reshape(lead + (n_heads * HEAD_DIM,))


def _rope_tables(seq_len):
    rows = seq_len // GRID_W
    row = jnp.repeat(jnp.arange(rows, dtype=F32), GRID_W)
    col = jnp.tile(jnp.arange(GRID_W, dtype=F32), rows)
    axis_dim = HEAD_DIM // 2
    omega = ROPE_THETA ** (-jnp.arange(0, axis_dim, 2, dtype=F32) / axis_dim)
    ang = jnp.concatenate([row[:, None] * omega[None], col[:, None] * omega[None]], axis=-1)
    c, s = jnp.cos(ang), jnp.sin(ang)
    return jnp.concatenate([c, c], axis=-1), jnp.concatenate([-s, s], axis=-1)


def _even_mixer(x, batch, seq_len, norm_g, w_in, short_w, short_b, f_w1, f_b1, f_w2, f_b2, f_w3, f_freq,
                hy_bias, q_g, k_g, w_out):
    d = x.shape[1]
    hy_d = hy_bias.shape[1]
    s0 = (HY_ORDER + 1) * hy_d
    nq = N_Q_HEADS * HEAD_DIM
    nk = N_KV_HEADS * HEAD_DIM
    g = norm_g.reshape(1, d)

    u_hy = _norm_proj(x, g, w_in[:, :s0].astype(BF16), F32, "hyena_in_proj")

    hid = LANES
    hpad = hid - f_w1.shape[1]
    zfeat = _position_features(seq_len, LANES)
    w1 = jnp.pad(f_w1, ((0, LANES - f_w1.shape[0]), (0, hpad)))
    b1 = jnp.pad(f_b1, (0, hpad)).reshape(1, hid)
    w2 = jnp.pad(f_w2, ((0, 0), (0, hpad), (0, hpad)))
    b2 = jnp.pad(f_b2, ((0, 0), (0, hpad))).reshape(HY_N_INNER, 1, hid)
    w3 = jnp.pad(f_w3, ((0, hpad), (0, 0)))
    freq = jnp.pad(f_freq, (0, hpad)).reshape(1, hid)
    max_decay = math.log(HY_DECAY_TARGET) / HY_FAST_DECAY
    min_decay = math.log(HY_DECAY_TARGET) / HY_SLOW_DECAY
    deltas = jnp.abs(jnp.linspace(min_decay, max_decay, hy_d, dtype=F32))
    deltas2 = jnp.tile(deltas, HY_ORDER).reshape(1, HY_ORDER * hy_d)
    ksum, kdiff = _hyena_filter_taps(zfeat, w1, b1, w2, b2, w3, freq, deltas2, seq_len, 256)
    p, pt = _dft_matrix(seq_len)
    ka, kc, ny = _hyena_spectrum(p, ksum, kdiff, 1024)
    bias = hy_bias.reshape(1, HY_ORDER * hy_d)
    sb = short_b.reshape(1, s0)
    z1 = _long_conv(u_hy, 0, u_hy, 1, short_w, sb, pt, ka, kc, ny, bias, 0, True, batch, seq_len, 256, F32)
    y_hy = _long_conv(z1, 0, u_hy, 2, short_w, sb, pt, ka, kc, ny, bias, 1, False, batch, seq_len, 256, BF16)

    w_qkv = jnp.concatenate([
        _deinterleave_heads(w_in[:, s0:s0 + nq], N_Q_HEADS),
        _deinterleave_heads(w_in[:, s0 + nq:s0 + nq + nk], N_KV_HEADS),
        w_in[:, s0 + nq + nk:]], axis=1).astype(BF16)
    qg = _deinterleave_heads(q_g, 1) * (HEAD_DIM ** -0.5 * LOG2E)
    kg = _deinterleave_heads(k_g, 1)
    head_gain = jnp.concatenate([jnp.tile(qg, N_Q_HEADS), jnp.tile(kg, N_KV_HEADS),
                                 jnp.ones((nk,), F32)]).reshape(1, nq + 2 * nk)
    cos, sin = _rope_tables(seq_len)
    qkv = _qkv_proj(x, g, w_qkv, head_gain, cos, sin, seq_len)
    y_at = _attention(qkv, batch, seq_len, min(1024, seq_len))

    return _out_proj(x, [y_hy, y_at], w_out.astype(BF16), "even_out_proj")


def _odd_mixer(x, batch, seq_len, norm_g, w_in, conv_w, conv_b, wa, ba, wx, bx, lam, w_out):
    d = x.shape[1]
    u = _norm_proj(x, norm_g.reshape(1, d), w_in.astype(BF16), F32, "rglru_in_proj", gelu_cols=w_in.shape[1] // 2)
    w_gates = (0.5 * jnp.concatenate([wa, wx], axis=-1)).astype(BF16)
    b_gates = 0.5 * jnp.concatenate([ba, bx], axis=-1)[:, :, None, :]
    y = _rglru(u, conv_w, conv_b, w_gates, b_gates, lam, batch, seq_len)
    return _out_proj(x, [y], w_out.astype(BF16), "rglru_out_proj")


def kernel(x_prompt, x_sample, ffn_norm, ffn_w_gate, ffn_w_up, ffn_w_down, even_norm, even_w_in, hy_short_w, hy_short_b, hy_filt_w1, hy_filt_b1, hy_filt_w2, hy_filt_b2, hy_filt_w3, hy_filt_freq, hy_bias, q_norm, k_norm, even_w_out, odd_norm, odd_w_in, rg_conv_w, rg_conv_b, rg_wa, rg_ba, rg_wx, rg_bx, rg_lambda, odd_w_out):
    bp, seq_len, d = x_prompt.shape
    bs = x_sample.shape[0]
    assert x_sample.shape[1:] == (seq_len, d)
    batch = bp + bs
    rows_p = bp * seq_len

    depth = ffn_norm.shape[0]
    ff = ffn_w_gate.shape[-1]
    fp = -(-ff // TF) * TF
    gate32 = ffn_w_gate.reshape(depth * 2, d, ff)
    up32 = ffn_w_up.reshape(depth * 2, d, ff)
    down32 = ffn_w_down.reshape(depth * 2, ff, d)
    weights = (_cast_pad(gate32, 1, d, fp, 1024, TF), _cast_pad(up32, 1, d, fp, 1024, TF),
               _cast_pad(down32, 1, fp, d, TF, d))
    fg = ffn_norm.reshape(depth * 2, 1, d)
    rows_s = bs * seq_len
    total = rows_p + rows_s

    def next_set(k):
        return (gate32, up32, down32, k) if k < 2 * depth else None

    x = None
    for layer in range(depth):
        if layer == 0:
            x = _ffn_half(x_prompt.reshape(rows_p, d), fg, 0, weights, out_rows=total)
            x, weights = _ffn_half(x_sample.reshape(rows_s, d), fg, 0, weights, out_rows=total, out_row0=rows_p,
                                   dest=x, cast_next=next_set(1))
        else:
            x, weights = _ffn_half(x, fg, 2 * layer, weights, cast_next=next_set(2 * layer + 1))
        j = layer // 2
        if layer % 2 == 0:
            x = _even_mixer(x, batch, seq_len, even_norm[j], even_w_in[j], hy_short_w[j], hy_short_b[j],
                            hy_filt_w1[j], hy_filt_b1[j], hy_filt_w2[j], hy_filt_b2[j], hy_filt_w3[j],
                            hy_filt_freq[j], hy_bias[j], q_norm[j], k_norm[j], even_w_out[j])
        else:
            x = _odd_mixer(x, batch, seq_len, odd_norm[j], odd_w_in[j], rg_conv_w[j], rg_conv_b[j],
                           rg_wa[j], rg_ba[j], rg_wx[j], rg_bx[j], rg_lambda[j], odd_w_out[j])
        if layer == depth - 1:
            y_prompt = _ffn_half(x, fg, 2 * layer + 1, weights, rows=rows_p)
            y_sample = _ffn_half(x, fg, 2 * layer + 1, weights, rows=rows_s, x_row0=rows_p)
        else:
            x, weights = _ffn_half(x, fg, 2 * layer + 1, weights, cast_next=next_set(2 * layer + 2))

    return (y_prompt.reshape(bp, seq_len, d), y_sample.reshape(bs, seq_len, d))
```
